```python
import jax, jax.numpy as jnp
from jax import lax
import numpy as np

D_MODEL = 1024
BATCH = 2
SEQ = 8192
DEPTH = 2

GRID_W = 64
CTX_LEN = 256
MIX_WIDTH = D_MODEL
ML_HEADS = 4
ML_WIDTH = MIX_WIDTH // 2
ML_V_DIM = ML_WIDTH // ML_HEADS
ML_QK_DIM = ML_V_DIM // 2
HG_WIDTH = MIX_WIDTH - ML_WIDTH
HG_EXPAND = 128
HG_HEADS = HG_WIDTH // HG_EXPAND
CHUNK = 64
CONV_K = 3
N_EXPERTS = 32
TOP_K = 4
D_EXPERT = D_MODEL
SWIGLU_LIMIT = 7.0
SWIGLU_ALPHA = 1.702
MOE_BLOCK = 256
EPS = 1e-6
IN_SPLITS = (2 * ML_HEADS * ML_QK_DIM, ML_WIDTH, ML_WIDTH, 4 * ML_HEADS, 2 * HG_WIDTH, HG_WIDTH, HG_WIDTH, HG_WIDTH)
IN_WIDTH = sum(IN_SPLITS)

kernel_name = 'hybrid_mlstm_hgrn2_moe_dit'


def rms_norm(x, g):
    xf = x.astype(jnp.float32)
    y = xf * lax.rsqrt(jnp.mean(xf * xf, axis=-1, keepdims=True) + EPS)
    return (y * g).astype(x.dtype)


def adaln(cond, w, b):
    mod = (jax.nn.silu(cond) @ w + b)[..., None, :]
    return jnp.split(mod, 6, axis=-1)


def modulate(h, shift, scale):
    return h * (1 + scale) + shift


def split_cols(p):
    return jnp.split(p, np.cumsum(IN_SPLITS)[:-1].tolist(), axis=-1)


def to_heads(t, n_heads):
    b, l, _ = t.shape
    return t.reshape(b, l, n_heads, -1).transpose(0, 2, 1, 3)


def from_heads(t):
    b, h, l, d = t.shape
    return t.transpose(0, 2, 1, 3).reshape(b, l, h * d)


def head_rms(h, g):
    h = h * lax.rsqrt(jnp.mean(h * h, axis=-1, keepdims=True) + EPS)
    return from_heads(h) * g


def grid_conv(t, w, width):
    b, l, ch = t.shape
    rows = l // width
    img = t.reshape(b, rows, width, ch)
    out = lax.conv_general_dilated(img, w[:, :, None, :], window_strides=(1, 1), padding='SAME',
                                   dimension_numbers=('NHWC', 'HWIO', 'NHWC'), feature_group_count=ch)
    return jax.nn.silu(out.reshape(b, l, ch))


def to_chunks(t):
    b, h, l = t.shape[:3]
    return jnp.moveaxis(t.reshape(b, h, l // CHUNK, CHUNK, *t.shape[3:]), 2, 0)


def from_chunks(t):
    t = jnp.moveaxis(t, 0, 2)
    b, h, nc, cl = t.shape[:4]
    return t.reshape(b, h, nc * cl, *t.shape[4:])


def mlstm_scan(q, k, v, ig, lf, state):
    tri = jnp.tril(jnp.ones((CHUNK, CHUNK), bool))

    def step(carry, inp):
        C, n, m = carry
        qc, kc, vc, ic, fc = inp
        b = jnp.cumsum(fc, axis=-1)
        a_inter = b + m[..., None]
        d = jnp.where(tri, b[..., :, None] - b[..., None, :] + ic[..., None, :], -jnp.inf)
        m_t = jnp.maximum(a_inter, d.max(-1))
        s = jnp.einsum('bhtd,bhsd->bhts', qc, kc) * jnp.exp(d - m_t[..., None])
        w_inter = jnp.exp(a_inter - m_t)
        num = s @ vc + w_inter[..., None] * jnp.einsum('bhtd,bhvd->bhtv', qc, C)
        den = s.sum(-1) + w_inter * jnp.einsum('bhtd,bhd->bht', qc, n)
        h = num / jnp.maximum(jnp.abs(den), jnp.exp(-m_t))[..., None]
        g = b[..., -1]
        e = g[..., None] - b + ic
        m_new = jnp.maximum(g + m, e.max(-1))
        we = jnp.exp(e - m_new[..., None])
        wc = jnp.exp(g + m - m_new)
        C = wc[..., None, None] * C + jnp.einsum('bhs,bhsv,bhsd->bhvd', we, vc, kc)
        n = wc[..., None] * n + jnp.einsum('bhs,bhsd->bhd', we, kc)
        return (C, n, m_new), h

    state, h = lax.scan(step, state, tuple(to_chunks(t) for t in (q, k, v, ig, lf)))
    return from_chunks(h), state


def hgrn2_scan(q, k, v, lf, S):
    tri = jnp.tril(jnp.ones((CHUNK, CHUNK), bool))[:, :, None]

    def step(S, inp):
        qc, kc, vc, fc = inp
        b = jnp.cumsum(fc, axis=2)
        decay = jnp.exp(jnp.where(tri, b[:, :, :, None, :] - b[:, :, None, :, :], -jnp.inf))
        a = jnp.einsum('bhtd,bhsd,bhtsd->bhts', qc, kc, decay)
        o = a @ vc + jnp.einsum('bhtd,bhdv->bhtv', qc * jnp.exp(b), S)
        g = b[:, :, -1:]
        S = jnp.exp(g[:, :, 0])[..., None] * S + jnp.einsum('bhsd,bhsv->bhdv', kc * jnp.exp(g - b), vc)
        return S, o

    S, o = lax.scan(step, S, tuple(to_chunks(t) for t in (q, k, v, lf)))
    return from_chunks(o), S


def bidirectional(scan_fn, ctx_fwd, ctx_bwd, lat_fwd, lat_bwd, state0):
    rev = lambda ts: tuple(jnp.flip(t, axis=2) for t in ts)
    hc_f, sc_f = scan_fn(*ctx_fwd, state0)
    hl_f, _ = scan_fn(*lat_fwd, sc_f)
    hc_b, sc_b = scan_fn(*rev(ctx_bwd), state0)
    hl_b, _ = scan_fn(*rev(lat_bwd), sc_b)
    return hc_f + jnp.flip(hc_b, axis=2), hl_f + jnp.flip(hl_b, axis=2)


def mlstm_prep(qk, v, gates, conv_w, gate_b, width):
    b, l, _ = v.shape
    qk = grid_conv(qk, conv_w, width).astype(jnp.float32)
    q, k = jnp.split(qk, 2, axis=-1)
    q = to_heads(q, ML_HEADS) * ML_QK_DIM ** -0.5
    k = to_heads(k, ML_HEADS)
    v = to_heads(v.astype(jnp.float32), ML_HEADS)
    g = (gates.astype(jnp.float32).reshape(b, l, 4, ML_HEADS) + gate_b).transpose(2, 0, 3, 1)
    fwd = (q, k, v, g[0], jax.nn.log_sigmoid(g[2]))
    bwd = (q, k, v, g[1], jax.nn.log_sigmoid(g[3]))
    return fwd, bwd


def mlstm_mixer(ctx_parts, lat_parts, conv_w, gate_b, norm_g):
    c_qk, c_v, c_o, c_g = ctx_parts
    l_qk, l_v, l_o, l_g = lat_parts
    c_fwd, c_bwd = mlstm_prep(c_qk, c_v, c_g, conv_w, gate_b, c_v.shape[1])
    l_fwd, l_bwd = mlstm_prep(l_qk, l_v, l_g, conv_w, gate_b, GRID_W)
    b = c_v.shape[0]
    state0 = (jnp.zeros((b, ML_HEADS, ML_V_DIM, ML_QK_DIM), jnp.float32),
              jnp.zeros((b, ML_HEADS, ML_QK_DIM), jnp.float32),
              jnp.zeros((b, ML_HEADS), jnp.float32))
    hc, hl = bidirectional(mlstm_scan, c_fwd, c_bwd, l_fwd, l_bwd, state0)
    yc = jax.nn.sigmoid(c_o.astype(jnp.float32)) * head_rms(hc, norm_g)
    yl = jax.nn.sigmoid(l_o.astype(jnp.float32)) * head_rms(hl, norm_g)
    return yc, yl


def hgrn2_prep(qi, ff, fb, conv_w, f_b, lb, width):
    qi = grid_conv(qi, conv_w, width).astype(jnp.float32)
    q, i = jnp.split(qi, 2, axis=-1)
    q = to_heads(q, HG_HEADS)
    v = to_heads(i, HG_HEADS)

    def forget(pre, bias):
        return to_heads(lb + (1 - lb) * jax.nn.sigmoid(pre.astype(jnp.float32) + bias), HG_HEADS)

    f_fwd = forget(ff, f_b[0])
    f_bwd = forget(fb, f_b[1])
    return (q, 1 - f_fwd, v, jnp.log(f_fwd)), (q, 1 - f_bwd, v, jnp.log(f_bwd))


def hgrn2_mixer(ctx_parts, lat_parts, conv_w, f_b, lb, norm_g):
    c_qi, c_ff, c_fb, c_go = ctx_parts
    l_qi, l_ff, l_fb, l_go = lat_parts
    c_fwd, c_bwd = hgrn2_prep(c_qi, c_ff, c_fb, conv_w, f_b, lb, c_qi.shape[1])
    l_fwd, l_bwd = hgrn2_prep(l_qi, l_ff, l_fb, conv_w, f_b, lb, GRID_W)
    b = c_qi.shape[0]
    state0 = jnp.zeros((b, HG_HEADS, HG_EXPAND, HG_WIDTH // HG_HEADS), jnp.float32)
    hc, hl = bidirectional(hgrn2_scan, c_fwd, c_bwd, l_fwd, l_bwd, state0)
    yc = jax.nn.silu(c_go.astype(jnp.float32)) * head_rms(hc, norm_g)
    yl = jax.nn.silu(l_go.astype(jnp.float32)) * head_rms(hl, norm_g)
    return yc, yl


def moe_ffn(h, router_w, router_b, w_gu, b_gu, w_down, b_down):
    t, d = h.shape
    logits = (h @ router_w + router_b).astype(jnp.float32)
    top_val, top_idx = lax.top_k(logits, TOP_K)
    gates = jax.nn.softmax(top_val, axis=-1)
    n = t * TOP_K
    n_blocks = -(-n // MOE_BLOCK) + N_EXPERTS
    flat_e = top_idx.reshape(-1)
    flat_tok = jnp.repeat(jnp.arange(t, dtype=jnp.int32), TOP_K)
    order = jnp.argsort(flat_e)
    sorted_e = flat_e[order]
    counts = jnp.bincount(flat_e, length=N_EXPERTS)
    padded = (counts + MOE_BLOCK - 1) // MOE_BLOCK * MOE_BLOCK
    start = jnp.cumsum(counts) - counts
    padded_end = jnp.cumsum(padded)
    pad_start = padded_end - padded
    dest = pad_start[sorted_e] + jnp.arange(n) - start[sorted_e]
    slot_tok = jnp.full((n_blocks * MOE_BLOCK,), t, jnp.int32).at[dest].set(flat_tok[order])
    slot_gate = jnp.zeros((n_blocks * MOE_BLOCK,), jnp.float32).at[dest].set(gates.reshape(-1)[order])
    block_e = jnp.minimum(jnp.searchsorted(padded_end, jnp.arange(n_blocks) * MOE_BLOCK, side='right'), N_EXPERTS - 1)
    h_pad = jnp.concatenate([h, jnp.zeros((1, d), h.dtype)], axis=0)
    xb = h_pad[slot_tok].reshape(n_blocks, MOE_BLOCK, d)

    def expert_block(args):
        xe, e = args
        gu = xe @ w_gu[e] + b_gu[e]
        glu, lin = jnp.split(gu, 2, axis=-1)
        glu = jnp.minimum(glu, SWIGLU_LIMIT)
        lin = jnp.clip(lin, -SWIGLU_LIMIT, SWIGLU_LIMIT)
        return (glu * jax.nn.sigmoid(SWIGLU_ALPHA * glu) * (lin + 1)) @ w_down[e] + b_down[e]

    yb = lax.map(expert_block, (xb, block_e)).reshape(n_blocks * MOE_BLOCK, d)
    out = jnp.zeros_like(h_pad).at[slot_tok].add(yb * slot_gate[:, None].astype(yb.dtype))
    return out[:t]


def setup_inputs(seed: int = 0) -> dict:
    key = jax.random.key(seed)
    ks = jax.random.split(key, 28)
    nrm = lambda k, shape, scale: jax.random.normal(k, shape, jnp.float32) * scale
    D = D_MODEL
    mlstm_gate_b = jnp.concatenate([
        nrm(ks[9], (DEPTH, 2, ML_HEADS), 0.1),
        jnp.linspace(3.0, 6.0, ML_HEADS) + nrm(ks[10], (DEPTH, 2, ML_HEADS), 0.1)], axis=1)
    return {
        'x': nrm(ks[0], (BATCH, SEQ, D), 1.0),
        'c': nrm(ks[1], (BATCH, D), 1.0),
        'ctx': nrm(ks[2], (BATCH, CTX_LEN, D), 1.0),
        'c_ctx': nrm(ks[3], (D,), 1.0),
        'w_ada': nrm(ks[4], (DEPTH, D, 6 * D), 0.5 * D ** -0.5),
        'b_ada': nrm(ks[5], (DEPTH, 6 * D), 0.02),
        'norm1_g': 1.0 + nrm(ks[6], (DEPTH, D), 0.02),
        'w_in': nrm(ks[7], (DEPTH, D, IN_WIDTH), D ** -0.5),
        'mlstm_conv': nrm(ks[8], (DEPTH, CONV_K, CONV_K, 2 * ML_HEADS * ML_QK_DIM), 1.0 / CONV_K),
        'mlstm_gate_b': mlstm_gate_b,
        'mlstm_norm_g': 1.0 + nrm(ks[11], (DEPTH, ML_WIDTH), 0.02),
        'hgrn_conv': nrm(ks[12], (DEPTH, CONV_K, CONV_K, 2 * HG_WIDTH), 1.0 / CONV_K),
        'hgrn_f_b': 1.0 + nrm(ks[13], (DEPTH, 2, HG_WIDTH), 0.5),
        'hgrn_lb_raw': nrm(ks[14], (DEPTH, HG_WIDTH), 0.5),
        'hgrn_norm_g': 1.0 + nrm(ks[15], (DEPTH, HG_WIDTH), 0.02),
        'w_out': nrm(ks[16], (DEPTH, MIX_WIDTH, D), MIX_WIDTH ** -0.5),
        'norm2_g': 1.0 + nrm(ks[17], (DEPTH, D), 0.02),
        'router_w': nrm(ks[18], (DEPTH, D, N_EXPERTS), D ** -0.5),
        'router_b': nrm(ks[19], (DEPTH, N_EXPERTS), 0.01),
        'w_gu': nrm(ks[20], (DEPTH, N_EXPERTS, D, 2 * D_EXPERT), D ** -0.5),
        'b_gu': nrm(ks[21], (DEPTH, N_EXPERTS, 2 * D_EXPERT), 0.02),
        'w_down': nrm(ks[22], (DEPTH, N_EXPERTS, D_EXPERT, D), D_EXPERT ** -0.5),
        'b_down': nrm(ks[23], (DEPTH, N_EXPERTS, D), 0.02),
        'final_g': 1.0 + nrm(ks[24], (D,), 0.02),
    }


def reference(x, c, ctx, c_ctx, w_ada, b_ada, norm1_g, w_in, mlstm_conv, mlstm_gate_b, mlstm_norm_g,
              hgrn_conv, hgrn_f_b, hgrn_lb_raw, hgrn_norm_g, w_out, norm2_g, router_w, router_b,
              w_gu, b_gu, w_down, b_down, final_g):
    bsz, seq, d = x.shape
    ctx_len = ctx.shape[1]
    lb_w = jax.nn.softmax(hgrn_lb_raw.astype(jnp.float32), axis=0)
    lower_bounds = jnp.cumsum(lb_w, axis=0) - lb_w[0]
    for l in range(DEPTH):
        last = l == DEPTH - 1
        sh1, sc1, g1, sh2, sc2, g2 = adaln(c, w_ada[l], b_ada[l])
        csh1, csc1, cg1, csh2, csc2, cg2 = adaln(c_ctx, w_ada[l], b_ada[l])
        pl = split_cols(modulate(rms_norm(x, norm1_g[l]), sh1, sc1) @ w_in[l])
        pc = split_cols(modulate(rms_norm(ctx, norm1_g[l]), csh1, csc1) @ w_in[l])
        yc_ml, yl_ml = mlstm_mixer(pc[:4], pl[:4], mlstm_conv[l], mlstm_gate_b[l], mlstm_norm_g[l])
        yc_hg, yl_hg = hgrn2_mixer(pc[4:], pl[4:], hgrn_conv[l], hgrn_f_b[l], lower_bounds[l], hgrn_norm_g[l])
        x = x + g1 * (jnp.concatenate([yl_ml, yl_hg], axis=-1).astype(x.dtype) @ w_out[l])
        hl2 = modulate(rms_norm(x, norm2_g[l]), sh2, sc2).reshape(bsz * seq, d)
        if last:
            x = x + g2 * moe_ffn(hl2, router_w[l], router_b[l], w_gu[l], b_gu[l], w_down[l], b_down[l]).reshape(x.shape)
        else:
            ctx = ctx + cg1 * (jnp.concatenate([yc_ml, yc_hg], axis=-1).astype(ctx.dtype) @ w_out[l])
            hc2 = modulate(rms_norm(ctx, norm2_g[l]), csh2, csc2).reshape(bsz * ctx_len, d)
            f = moe_ffn(jnp.concatenate([hc2, hl2], axis=0), router_w[l], router_b[l], w_gu[l], b_gu[l], w_down[l], b_down[l])
            ctx = ctx + cg2 * f[:bsz * ctx_len].reshape(ctx.shape)
            x = x + g2 * f[bsz * ctx_len:].reshape(x.shape)
    return rms_norm(x, final_g)
```

```python
import functools

import jax
import jax.numpy as jnp
from jax import lax
from jax.experimental import pallas as pl
from jax.experimental.pallas import tpu as pltpu

F32 = jnp.float32
BF16 = jnp.bfloat16
HIGHEST = lax.Precision.HIGHEST

GRID_W = 64
CHUNK = 64
ML_HEADS = 4
ML_QK = 64
ML_V = 128
HG_HEADS = 4
HG_DK = 128
N_EXPERTS = 32
TOP_K = 4
SWIGLU_LIMIT = 7.0
SWIGLU_ALPHA = 1.702
EPS = 1e-6

TILE = 256
MOE_BLK = 256
LANES = 128
SUB = 16
NEG = -1e30
VMEM_LIMIT = 48 * 1024 * 1024


def _cparams(n_axes, vmem=VMEM_LIMIT):
    return pltpu.CompilerParams(dimension_semantics=("arbitrary",) * n_axes, vmem_limit_bytes=vmem)


def _dot(a, b):
    return jnp.dot(a, b, preferred_element_type=F32)


def _dot_nt(a, b):
    return lax.dot_general(a, b, (((1,), (1,)), ((), ())), preferred_element_type=F32)


def _dot_tn(a, b):
    return lax.dot_general(a, b, (((0,), (0,)), ((), ())), preferred_element_type=F32)


def _dot_hi(a, b):
    return jnp.dot(a, b, precision=HIGHEST, preferred_element_type=F32)


def _rms(x, g):
    return x * lax.rsqrt(jnp.mean(x * x, axis=-1, keepdims=True) + EPS) * g


def _sigmoid(x):
    return 1.0 / (1.0 + jnp.exp(-x))


def _log_sigmoid(x):
    return jnp.minimum(x, 0.0) - jnp.log(1.0 + jnp.exp(-jnp.abs(x)))


def _ada_kernel(cond_ref, w_ref, b_ref, o_ref):
    c = cond_ref[...]
    o_ref[...] = _dot_hi(c * _sigmoid(c), w_ref[...]) + b_ref[...]


def _ada_call(cond, w_ada, b_ada):
    depth, d, n6 = w_ada.shape
    tn = 1024
    return pl.pallas_call(
        _ada_kernel,
        grid=(depth, n6 // tn),
        in_specs=[
            pl.BlockSpec((8, d), lambda l, j: (0, 0)),
            pl.BlockSpec((None, d, tn), lambda l, j: (l, 0, j)),
            pl.BlockSpec((None, 1, tn), lambda l, j: (l, 0, j)),
        ],
        out_specs=pl.BlockSpec((None, 8, tn), lambda l, j: (l, 0, j)),
        out_shape=jax.ShapeDtypeStruct((depth, 8, n6), F32),
        compiler_params=_cparams(2),
        name="ada",
    )(cond, w_ada, b_ada.reshape(depth, 1, n6))


def _mod_row(mod_ref, tpb, n_batch):
    i = pl.program_id(0)
    row = jnp.where(i % tpb == 0, n_batch, i // tpb)
    return mod_ref[pl.ds(row, 1), :]


IN_SEGS = (512, 512, 512, 128, 1024, 512, 512, 512)


def _inproj_kernel(x_ref, mod_ref, g_ref, w_ref, *outs, tpb, n_batch):
    d = x_ref.shape[1]
    mod = _mod_row(mod_ref, tpb, n_batch)
    h = _rms(x_ref[...], g_ref[...]) * (1.0 + mod[:, d:2 * d]) + mod[:, 0:d]
    hb = h.astype(BF16)
    off = 0
    for o_ref, n in zip(outs, IN_SEGS):
        o_ref[...] = _dot(hb, w_ref[:, off:off + n])
        off += n


def _inproj_call(x, mod_l, g, w_p, tpb, n_batch):
    n, d = x.shape
    nw = w_p.shape[1]
    return pl.pallas_call(
        functools.partial(_inproj_kernel, tpb=tpb, n_batch=n_batch),
        grid=(n // TILE,),
        in_specs=[
            pl.BlockSpec((TILE, d), lambda i: (i, 0)),
            pl.BlockSpec((8, 6 * d), lambda i: (0, 0)),
            pl.BlockSpec((1, d), lambda i: (0, 0)),
            pl.BlockSpec((d, nw), lambda i: (0, 0)),
        ],
        out_specs=[pl.BlockSpec((TILE, s), lambda i: (i, 0)) for s in IN_SEGS],
        out_shape=[jax.ShapeDtypeStruct((n, s), F32) for s in IN_SEGS],
        compiler_params=_cparams(1),
        name="inproj",
    )(x, mod_l, g.reshape(1, d), w_p)


def _conv_kernel(qk_m, qk_p, qk_n, qi_m, qi_p, qi_n, wq_ref, wi_ref, q_ref, k_ref, hq_ref, hv_ref, *, tpb):
    r = lax.broadcasted_iota(jnp.int32, (TILE, 1), 0)
    j = jnp.zeros((TILE, 1), jnp.int32) + pl.program_id(0) % tpb
    is_ctx = j == 0
    col = r % GRID_W
    ok_l = jnp.logical_or(col != 0, jnp.logical_and(is_ctx, r != 0))
    ok_r = jnp.logical_or(col != GRID_W - 1, jnp.logical_and(is_ctx, r != TILE - 1))
    ok_u = jnp.logical_and(jnp.logical_not(is_ctx), jnp.logical_or(j != 1, r >= GRID_W))
    ok_d = jnp.logical_and(jnp.logical_not(is_ctx), jnp.logical_or(j != tpb - 1, r < TILE - GRID_W))
    col_ok = (ok_l, None, ok_r)
    row_ok = (ok_u, None, ok_d)
    n_ext = TILE + 2 * GRID_W

    def conv(main, prev, nxt, w_ref, c0, c1):
        ext = jnp.concatenate([prev[:, c0:c1], main[:, c0:c1], nxt[:, c0:c1]], axis=0)
        shifted = (pltpu.roll(ext, 1, 0), ext, pltpu.roll(ext, n_ext - 1, 0))
        acc = jnp.zeros((TILE, c1 - c0), F32)
        for kh in range(3):
            for kw in range(3):
                tap = shifted[kw][kh * GRID_W:kh * GRID_W + TILE]
                ok = None
                for m in (row_ok[kh], col_ok[kw]):
                    if m is not None:
                        ok = m if ok is None else jnp.logical_and(ok, m)
                if ok is not None:
                    tap = jnp.where(ok, tap, 0.0)
                acc = acc + tap * w_ref[kh * 3 + kw:kh * 3 + kw + 1, c0:c1]
        return acc * _sigmoid(acc)

    nq = ML_HEADS * ML_QK
    q_ref[...] = conv(qk_m, qk_p, qk_n, wq_ref, 0, nq) * (ML_QK ** -0.5)
    k_ref[...] = conv(qk_m, qk_p, qk_n, wq_ref, nq, 2 * nq)
    nh = HG_HEADS * HG_DK
    hq_ref[...] = conv(qi_m, qi_p, qi_n, wi_ref, 0, nh)
    hv_ref[...] = conv(qi_m, qi_p, qi_n, wi_ref, nh, 2 * nh)


def _conv_call(qk, qi, wq, wi, tpb):
    n = qk.shape[0]
    rpt = TILE // GRID_W
    nrow = n // GRID_W
    cq, ci = qk.shape[1], qi.shape[1]

    def main(i):
        return (i, 0)

    def prev(i):
        return (jnp.maximum(i * rpt - 1, 0), 0)

    def nxt(i):
        return (jnp.minimum((i + 1) * rpt, nrow - 1), 0)

    return pl.pallas_call(
        functools.partial(_conv_kernel, tpb=tpb),
        grid=(n // TILE,),
        in_specs=[
            pl.BlockSpec((TILE, cq), main), pl.BlockSpec((GRID_W, cq), prev), pl.BlockSpec((GRID_W, cq), nxt),
            pl.BlockSpec((TILE, ci), main), pl.BlockSpec((GRID_W, ci), prev), pl.BlockSpec((GRID_W, ci), nxt),
            pl.BlockSpec((9, cq), lambda i: (0, 0)),
            pl.BlockSpec((9, ci), lambda i: (0, 0)),
        ],
        out_specs=[pl.BlockSpec((TILE, s), main) for s in (cq // 2, cq // 2, ci // 2, ci // 2)],
        out_shape=[jax.ShapeDtypeStruct((n, s), F32) for s in (cq // 2, cq // 2, ci // 2, ci // 2)],
        compiler_params=_cparams(1),
        name="conv",
    )(qk, qk, qk, qi, qi, qi, wq.reshape(9, cq), wi.reshape(9, ci))


def _chunk_maps(ncc, nc):
    def fwd(b, j):
        return (b * nc + j, 0)

    def bwd(b, j):
        return (b * nc + jnp.where(j < ncc, ncc - 1 - j, nc + ncc - 1 - j), 0)

    return fwd, bwd


def _tri(fwd):
    t = lax.broadcasted_iota(jnp.int32, (CHUNK, CHUNK), 0)
    s = lax.broadcasted_iota(jnp.int32, (CHUNK, CHUNK), 1)
    return (s <= t) if fwd else (s >= t)


def _mlstm_kernel(qf, kf, vf, gf, qb, kb, vb, gb, bias_ref, hf_ref, hb_ref, ct_ref, m_ref):
    @pl.when(pl.program_id(1) == 0)
    def _():
        ct_ref[...] = jnp.zeros_like(ct_ref)
        m_ref[...] = jnp.zeros_like(m_ref)

    lane = lax.broadcasted_iota(jnp.int32, (CHUNK, LANES), 1)
    ones_col = jnp.where(lane == 0, 1.0, 0.0).astype(BF16)
    for d, (q_ref, k_ref, v_ref, g_ref, h_ref) in enumerate(((qf, kf, vf, gf, hf_ref), (qb, kb, vb, gb, hb_ref))):
        fwd = d == 0
        mask = _tri(fwd)
        g_all = g_ref[...] + bias_ref[...]
        bcum = _dot_hi(mask.astype(F32), _log_sigmoid(g_all))
        g_t = g_all.T
        b_t = bcum.T
        last = CHUNK - 1 if fwd else 0
        for h in range(ML_HEADS):
            ci = d * ML_HEADS + h
            cf = (2 + d) * ML_HEADS + h
            b_col, b_row = bcum[:, cf:cf + 1], b_t[cf:cf + 1, :]
            i_col, i_row = g_all[:, ci:ci + 1], g_t[ci:ci + 1, :]
            m = m_ref[ci:ci + 1, 0:1]
            dmat = jnp.where(mask, b_col - b_row + i_row, -jnp.inf)
            a_inter = b_col + m
            m_t = jnp.maximum(a_inter, jnp.max(dmat, axis=-1, keepdims=True))
            q = q_ref[:, h * ML_QK:(h + 1) * ML_QK]
            k = k_ref[:, h * ML_QK:(h + 1) * ML_QK]
            qbf = q.astype(BF16)
            s = _dot_nt(qbf, k.astype(BF16)) * jnp.exp(dmat - m_t)
            w_inter = jnp.exp(a_inter - m_t)
            vaug = jnp.concatenate([v_ref[:, h * ML_V:(h + 1) * ML_V].astype(BF16), ones_col], axis=1)
            ct = ct_ref[ci]
            nd = _dot(s.astype(BF16), vaug) + w_inter * _dot(qbf, ct.astype(BF16))
            den = nd[:, ML_V:ML_V + 1]
            h_ref[:, h * ML_V:(h + 1) * ML_V] = nd[:, 0:ML_V] / jnp.maximum(jnp.abs(den), jnp.exp(-m_t))
            g_tot = b_col[last:last + 1, :]
            e_col = g_tot - b_col + i_col
            e_row = g_tot - b_row + i_row
            m_new = jnp.maximum(g_tot + m, jnp.max(e_row, axis=-1, keepdims=True))
            kw = (k * jnp.exp(e_col - m_new)).astype(BF16)
            ct_ref[ci] = jnp.exp(g_tot + m - m_new) * ct + _dot_tn(kw, vaug)
            m_ref[ci:ci + 1, :] = jnp.broadcast_to(m_new, (1, LANES))


def _mlstm_call(q, k, v, gates, gate_b, n_batch, ncc):
    n = q.shape[0]
    nc = n // CHUNK // n_batch
    fwd, bwd = _chunk_maps(ncc, nc)
    dq, dv = q.shape[1], v.shape[1]
    specs = lambda m: [pl.BlockSpec((CHUNK, dq), m), pl.BlockSpec((CHUNK, dq), m),
                       pl.BlockSpec((CHUNK, dv), m), pl.BlockSpec((CHUNK, LANES), m)]
    bias = jnp.zeros((1, LANES), F32).at[0, :4 * ML_HEADS].set(gate_b.reshape(-1))
    return pl.pallas_call(
        _mlstm_kernel,
        grid=(n_batch, nc),
        in_specs=specs(fwd) + specs(bwd) + [pl.BlockSpec((1, LANES), lambda b, j: (0, 0))],
        out_specs=[pl.BlockSpec((CHUNK, dv), fwd), pl.BlockSpec((CHUNK, dv), bwd)],
        out_shape=[jax.ShapeDtypeStruct((n, dv), F32)] * 2,
        scratch_shapes=[pltpu.VMEM((2 * ML_HEADS, ML_QK, ML_V + LANES), F32), pltpu.VMEM((2 * ML_HEADS, LANES), F32)],
        compiler_params=_cparams(2),
        name="mlstm",
    )(q, k, v, gates, q, k, v, gates, bias)


def _hgrn_kernel(qf, vf, ff, qb, vb, fb, fbias_ref, lb_ref, of_ref, ob_ref, st_ref):
    @pl.when(pl.program_id(1) == 0)
    def _():
        st_ref[...] = jnp.zeros_like(st_ref)

    n_sub = CHUNK // SUB
    rows = lax.broadcasted_iota(jnp.int32, (CHUNK, 1), 0)
    t_i = lax.broadcasted_iota(jnp.int32, (CHUNK, CHUNK), 0)
    s_i = lax.broadcasted_iota(jnp.int32, (CHUNK, CHUNK), 1)
    same_sub = (t_i // SUB) == (s_i // SUB)
    lb = lb_ref[...]
    for d, (q_ref, v_ref, f_ref, o_ref) in enumerate(((qf, vf, ff, of_ref), (qb, vb, fb, ob_ref))):
        fwd = d == 0
        mask = _tri(fwd)
        f = lb + (1.0 - lb) * _sigmoid(f_ref[...] + fbias_ref[d:d + 1, :])
        kk = 1.0 - f
        b = _dot_hi(mask.astype(F32), jnp.log(f))
        q = q_ref[...]
        q_parts, k_parts = [], []
        for p in range(1, n_sub):
            if fwd:
                ref_row, q_ok, k_ok = p * SUB - 1, (rows // SUB) == p, rows < p * SUB
            else:
                ref_row, q_ok, k_ok = p * SUB, (rows // SUB) == p - 1, rows >= p * SUB
            r_p = b[ref_row:ref_row + 1, :]
            q_parts.append(jnp.where(q_ok, q * jnp.exp(jnp.minimum(b - r_p, 0.0)), 0.0))
            k_parts.append(jnp.where(k_ok, kk * jnp.exp(jnp.minimum(r_p - b, 0.0)), 0.0))
        mid = SUB // 2 - 1 if fwd else SUB // 2
        c_m = b[mid:mid + 1, :]
        for i in range(1, n_sub):
            c_m = jnp.where(rows >= i * SUB, b[i * SUB + mid:i * SUB + mid + 1, :], c_m)
        q_d = q * jnp.exp(b - c_m)
        k_d = kk * jnp.exp(c_m - b)
        last = CHUNK - 1 if fwd else 0
        g_row = b[last:last + 1, :]
        q_in = q * jnp.exp(b)
        k_out = kk * jnp.exp(g_row - b)
        decay = jnp.exp(g_row)
        diag_ok = jnp.logical_and(same_sub, mask)
        for h in range(HG_HEADS):
            sl = slice(h * HG_DK, (h + 1) * HG_DK)
            qc = jnp.concatenate([x[:, sl] for x in q_parts], axis=1).astype(BF16)
            kc = jnp.concatenate([x[:, sl] for x in k_parts], axis=1).astype(BF16)
            a = _dot_nt(qc, kc) + jnp.where(diag_ok, _dot_nt(q_d[:, sl].astype(BF16), k_d[:, sl].astype(BF16)), 0.0)
            vh = v_ref[:, sl].astype(BF16)
            st = st_ref[d * HG_HEADS + h]
            o_ref[:, sl] = _dot(a.astype(BF16), vh) + _dot_nt(q_in[:, sl].astype(BF16), st.astype(BF16))
            st_ref[d * HG_HEADS + h] = st * decay[:, sl] + _dot_tn(vh, k_out[:, sl].astype(BF16))


def _hgrn_call(hq, hv, ff, fb, f_b, lb, n_batch, ncc):
    n, w = hq.shape
    nc = n // CHUNK // n_batch
    fwd, bwd = _chunk_maps(ncc, nc)
    spec = lambda m: pl.BlockSpec((CHUNK, w), m)
    return pl.pallas_call(
        _hgrn_kernel,
        grid=(n_batch, nc),
        in_specs=[spec(fwd)] * 3 + [spec(bwd)] * 3 + [pl.BlockSpec((2, w), lambda b, j: (0, 0)),
                                                      pl.BlockSpec((1, w), lambda b, j: (0, 0))],
        out_specs=[spec(fwd), spec(bwd)],
        out_shape=[jax.ShapeDtypeStruct((n, w), F32)] * 2,
        scratch_shapes=[pltpu.VMEM((2 * HG_HEADS, HG_DK, HG_DK), F32)],
        compiler_params=_cparams(2),
        name="hgrn",
    )(hq, hv, ff, hq, hv, fb, f_b, lb.reshape(1, w))


def _head_rms(h, g, width):
    parts = []
    for i in range(h.shape[1] // width):
        p = h[:, i * width:(i + 1) * width]
        parts.append(p * lax.rsqrt(jnp.mean(p * p, axis=-1, keepdims=True) + EPS))
    return jnp.concatenate(parts, axis=1) * g


def _out_kernel(x_ref, mlf, mlb, hgf, hgb, o_ref, go_ref, mod_ref, gml_ref, ghg_ref, wout_ref, g2_ref, rw_ref, rb_ref,
                xo_ref, h2_ref, idx_ref, gate_ref, *, tpb, n_batch):
    d = x_ref.shape[1]
    mod = _mod_row(mod_ref, tpb, n_batch)
    y_ml = _sigmoid(o_ref[...]) * _head_rms(mlf[...] + mlb[...], gml_ref[...], ML_V)
    go = go_ref[...]
    y_hg = go * _sigmoid(go) * _head_rms(hgf[...] + hgb[...], ghg_ref[...], HG_DK)
    y = jnp.concatenate([y_ml, y_hg], axis=1).astype(BF16)
    x = x_ref[...] + mod[:, 2 * d:3 * d] * _dot(y, wout_ref[...])
    xo_ref[...] = x
    h2 = _rms(x, g2_ref[...]) * (1.0 + mod[:, 4 * d:5 * d]) + mod[:, 3 * d:4 * d]
    h2_ref[...] = h2
    vals = _dot_hi(h2, rw_ref[...]) + rb_ref[...]
    lane = lax.broadcasted_iota(jnp.int32, vals.shape, 1)
    lane_f = lane.astype(F32)
    idx_out = jnp.zeros(vals.shape, F32)
    top = []
    for k in range(TOP_K):
        mx = jnp.max(vals, axis=-1, keepdims=True)
        ix = jnp.min(jnp.where(vals == mx, lane_f, float(LANES)), axis=-1, keepdims=True)
        top.append(mx)
        idx_out = jnp.where(lane == k, ix, idx_out)
        vals = jnp.where(lane_f == ix, -jnp.inf, vals)
    ex = [jnp.exp(t - top[0]) for t in top]
    tot = ex[0] + ex[1] + ex[2] + ex[3]
    gate_out = jnp.zeros(vals.shape, F32)
    for k in range(TOP_K):
        gate_out = jnp.where(lane == k, ex[k] / tot, gate_out)
    idx_ref[...] = idx_out.astype(jnp.int32)
    gate_ref[...] = gate_out


def _out_call(x, mlf, mlb, hgf, hgb, o, go, mod_l, gml, ghg, wout, g2, rw, rb, tpb, n_batch):
    n, d = x.shape
    w = mlf.shape[1]
    row = lambda i: (i, 0)
    fix = lambda i: (0, 0)
    rw_p = jnp.zeros((d, LANES), F32).at[:, :N_EXPERTS].set(rw)
    rb_p = jnp.full((1, LANES), NEG, F32).at[0, :N_EXPERTS].set(rb)
    return pl.pallas_call(
        functools.partial(_out_kernel, tpb=tpb, n_batch=n_batch),
        grid=(n // TILE,),
        in_specs=[pl.BlockSpec((TILE, d), row)] + [pl.BlockSpec((TILE, w), row)] * 6 + [
            pl.BlockSpec((8, 6 * d), fix), pl.BlockSpec((1, w), fix), pl.BlockSpec((1, w), fix),
            pl.BlockSpec((d, d), fix), pl.BlockSpec((1, d), fix), pl.BlockSpec((d, LANES), fix),
            pl.BlockSpec((1, LANES), fix)],
        out_specs=[pl.BlockSpec((TILE, d), row), pl.BlockSpec((TILE, d), row),
                   pl.BlockSpec((TILE, LANES), row), pl.BlockSpec((TILE, LANES), row)],
        out_shape=[jax.ShapeDtypeStruct((n, d), F32), jax.ShapeDtypeStruct((n, d), F32),
                   jax.ShapeDtypeStruct((n, LANES), jnp.int32), jax.ShapeDtypeStruct((n, LANES), F32)],
        compiler_params=_cparams(1),
        name="outproj",
    )(x, mlf, mlb, hgf, hgb, o, go, mod_l, gml.reshape(1, w), ghg.reshape(1, w), wout, g2.reshape(1, d), rw_p, rb_p)


def _onehots(idx):
    lane = lax.broadcasted_iota(jnp.int32, idx.shape, 1)
    return [lane == idx[:, k:k + 1] for k in range(TOP_K)]


def _rank_kernel(idx_ref, rank_ref, cnt_ref, carry_ref):
    @pl.when(pl.program_id(0) == 0)
    def _():
        carry_ref[...] = jnp.zeros_like(carry_ref)

    hots = _onehots(idx_ref[...])
    m = jnp.zeros(idx_ref.shape, F32)
    for hk in hots:
        m = m + jnp.where(hk, 1.0, 0.0)
    t_i = lax.broadcasted_iota(jnp.int32, (TILE, TILE), 0)
    s_i = lax.broadcasted_iota(jnp.int32, (TILE, TILE), 1)
    before = _dot((s_i < t_i).astype(BF16), m.astype(BF16)) + carry_ref[...]
    lane = lax.broadcasted_iota(jnp.int32, idx_ref.shape, 1)
    out = jnp.zeros(idx_ref.shape, F32)
    for k, hk in enumerate(hots):
        rk = jnp.sum(jnp.where(hk, before, 0.0), axis=-1, keepdims=True)
        out = jnp.where(lane == k, rk, out)
    rank_ref[...] = out.astype(jnp.int32)
    carry_ref[...] = carry_ref[...] + jnp.sum(m, axis=0, keepdims=True)
    cnt_ref[...] = carry_ref[...]


def _rank_call(idx):
    n = idx.shape[0]
    return pl.pallas_call(
        _rank_kernel,
        grid=(n // TILE,),
        in_specs=[pl.BlockSpec((TILE, LANES), lambda i: (i, 0))],
        out_specs=[pl.BlockSpec((TILE, LANES), lambda i: (i, 0)), pl.BlockSpec((1, LANES), lambda i: (0, 0))],
        out_shape=[jax.ShapeDtypeStruct((n, LANES), jnp.int32), jax.ShapeDtypeStruct((1, LANES), F32)],
        scratch_shapes=[pltpu.VMEM((1, LANES), F32)],
        compiler_params=_cparams(1),
        name="rank",
    )(idx)


def _dest_kernel(idx_ref, rank_ref, cnt_ref, dest_ref, meta_ref, *, n_blocks):
    cnt = cnt_ref[...]
    padded = jnp.floor((cnt + (MOE_BLK - 1)) * (1.0 / MOE_BLK)) * MOE_BLK
    e_i = lax.broadcasted_iota(jnp.int32, (LANES, LANES), 0)
    e_j = lax.broadcasted_iota(jnp.int32, (LANES, LANES), 1)
    pad_start = _dot_hi(jnp.broadcast_to(padded, (8, LANES)), (e_i < e_j).astype(F32))[0:1]
    pad_end = pad_start + padded
    lane = lax.broadcasted_iota(jnp.int32, idx_ref.shape, 1)
    rank = rank_ref[...]
    out = jnp.zeros(idx_ref.shape, jnp.int32)
    for k, hk in enumerate(_onehots(idx_ref[...])):
        st = jnp.sum(jnp.where(hk, pad_start, 0.0), axis=-1, keepdims=True).astype(jnp.int32)
        out = jnp.where(lane == k, st + rank[:, k:k + 1], out)
    dest_ref[...] = out
    blk = lax.broadcasted_iota(jnp.int32, meta_ref.shape, 0).astype(F32) * MOE_BLK
    lane_m = lax.broadcasted_iota(jnp.int32, meta_ref.shape, 1)
    done = jnp.where(jnp.logical_and(lane_m < N_EXPERTS, pad_end <= blk), 1.0, 0.0)
    be = jnp.minimum(jnp.sum(done, axis=-1, keepdims=True), N_EXPERTS - 1.0)
    used = jnp.sum(jnp.where(lane_m[0:1] < N_EXPERTS, padded, 0.0), axis=-1, keepdims=True) * (1.0 / MOE_BLK)
    meta_ref[...] = jnp.where(lane_m == 0, be, jnp.where(lane_m == 1, used, 0.0)).astype(jnp.int32)


def _dest_call(idx, rank, cnt, n_blocks):
    n = idx.shape[0]
    nb_pad = -(-n_blocks // 8) * 8
    row = lambda i: (i, 0)
    return pl.pallas_call(
        functools.partial(_dest_kernel, n_blocks=n_blocks),
        grid=(n // TILE,),
        in_specs=[pl.BlockSpec((TILE, LANES), row), pl.BlockSpec((TILE, LANES), row),
                  pl.BlockSpec((1, LANES), lambda i: (0, 0))],
        out_specs=[pl.BlockSpec((TILE, LANES), row), pl.BlockSpec((nb_pad, LANES), lambda i: (0, 0))],
        out_shape=[jax.ShapeDtypeStruct((n, LANES), jnp.int32), jax.ShapeDtypeStruct((nb_pad, LANES), jnp.int32)],
        compiler_params=_cparams(1),
        name="dest",
    )(idx, rank, cnt)


def _scatter_kernel(dest_ref, h_ref, xb_in, xb_ref, sem):
    del xb_in

    def row_copy(p):
        return pltpu.make_async_copy(h_ref.at[pl.ds(p // TOP_K, 1), :], xb_ref.at[pl.ds(dest_ref[p], 1), :], sem)

    def start(p, c):
        row_copy(p).start()
        return c

    def wait(p, c):
        row_copy(p).wait()
        return c

    lax.fori_loop(0, TILE * TOP_K, start, 0)
    lax.fori_loop(0, TILE * TOP_K, wait, 0)


def _scatter_call(dest_flat, h2, n_rows):
    n, d = h2.shape
    xb0 = jnp.zeros((n_rows, d), F32)
    return pl.pallas_call(
        _scatter_kernel,
        grid=(n // TILE,),
        in_specs=[pl.BlockSpec((TILE * TOP_K,), lambda i: (i,), memory_space=pltpu.SMEM),
                  pl.BlockSpec((TILE, d), lambda i: (i, 0)),
                  pl.BlockSpec(memory_space=pl.ANY)],
        out_specs=pl.BlockSpec(memory_space=pl.ANY),
        out_shape=jax.ShapeDtypeStruct((n_rows, d), F32),
        scratch_shapes=[pltpu.SemaphoreType.DMA(())],
        input_output_aliases={2: 0},
        compiler_params=_cparams(1),
        name="scatter",
    )(dest_flat, h2, xb0)


def _expert_kernel(be_ref, used_ref, x_ref, wgu_ref, bgu_ref, wd_ref, bd_ref, y_ref):
    live = pl.program_id(0) < used_ref[0]

    @pl.when(jnp.logical_not(live))
    def _():
        y_ref[...] = jnp.zeros_like(y_ref)

    @pl.when(live)
    def _():
        de = wd_ref.shape[0]
        gu = _dot(x_ref[...].astype(BF16), wgu_ref[...]) + bgu_ref[...]
        glu = jnp.minimum(gu[:, :de], SWIGLU_LIMIT)
        lin = jnp.clip(gu[:, de:], -SWIGLU_LIMIT, SWIGLU_LIMIT)
        act = glu * _sigmoid(SWIGLU_ALPHA * glu) * (lin + 1.0)
        y_ref[...] = _dot(act.astype(BF16), wd_ref[...]) + bd_ref[...]


def _expert_call(be, used, xb, wgu, bgu, wd, bd):
    n_rows, d = xb.shape
    e, _, de2 = wgu.shape
    nb = n_rows // MOE_BLK
    blk = lambda i, be, used: (jnp.minimum(i, used[0] - 1), 0)
    wsel = lambda i, be, used: (be[jnp.minimum(i, used[0] - 1)], 0, 0)
    return pl.pallas_call(
        _expert_kernel,
        grid_spec=pltpu.PrefetchScalarGridSpec(
            num_scalar_prefetch=2,
            grid=(nb,),
            in_specs=[pl.BlockSpec((MOE_BLK, d), blk),
                      pl.BlockSpec((None, d, de2), wsel), pl.BlockSpec((None, 1, de2), wsel),
                      pl.BlockSpec((None, de2 // 2, d), wsel), pl.BlockSpec((None, 1, d), wsel)],
            out_specs=pl.BlockSpec((MOE_BLK, d), lambda i, be, used: (i, 0)),
        ),
        out_shape=jax.ShapeDtypeStruct((n_rows, d), F32),
        compiler_params=_cparams(1),
        name="expert",
    )(be, used, xb, wgu, bgu.reshape(e, 1, de2), wd, bd.reshape(e, 1, d))


def _combine_kernel(dest_ref, x_ref, gate_ref, mod_ref, fg_ref, yb_ref, o_ref, buf, sem, *, tpb, n_batch, final):
    def row_copy(p):
        return pltpu.make_async_copy(yb_ref.at[pl.ds(dest_ref[p], 1), :],
                                     buf.at[p % TOP_K, pl.ds(p // TOP_K, 1), :], sem)

    def start(p, c):
        row_copy(p).start()
        return c

    def wait(p, c):
        row_copy(p).wait()
        return c

    lax.fori_loop(0, TILE * TOP_K, start, 0)
    lax.fori_loop(0, TILE * TOP_K, wait, 0)
    d = x_ref.shape[1]
    mod = _mod_row(mod_ref, tpb, n_batch)
    gates = gate_ref[...]
    f = gates[:, 0:1] * buf[0]
    for k in range(1, TOP_K):
        f = f + gates[:, k:k + 1] * buf[k]
    x = x_ref[...] + mod[:, 5 * d:6 * d] * f
    o_ref[...] = _rms(x, fg_ref[...]) if final else x


def _combine_call(dest_flat, x, gates, mod_l, fg, yb, tpb, n_batch, final):
    n, d = x.shape
    if final:
        out_rows = n - n_batch * TILE
        omap = lambda i: ((i // tpb) * (tpb - 1) + jnp.maximum(i % tpb - 1, 0), 0)
    else:
        out_rows = n
        omap = lambda i: (i, 0)
    return pl.pallas_call(
        functools.partial(_combine_kernel, tpb=tpb, n_batch=n_batch, final=final),
        grid=(n // TILE,),
        in_specs=[pl.BlockSpec((TILE * TOP_K,), lambda i: (i,), memory_space=pltpu.SMEM),
                  pl.BlockSpec((TILE, d), lambda i: (i, 0)),
                  pl.BlockSpec((TILE, LANES), lambda i: (i, 0)),
                  pl.BlockSpec((8, 6 * d), lambda i: (0, 0)),
                  pl.BlockSpec((1, d), lambda i: (0, 0)),
                  pl.BlockSpec(memory_space=pl.ANY)],
        out_specs=pl.BlockSpec((TILE, d), omap),
        out_shape=jax.ShapeDtypeStruct((out_rows, d), F32),
        scratch_shapes=[pltpu.VMEM((TOP_K, TILE, d), F32), pltpu.SemaphoreType.DMA(())],
        compiler_params=_cparams(1),
        name="combine",
    )(dest_flat, x, gates, mod_l, fg.reshape(1, d), yb)


def kernel(x, c, ctx, c_ctx, w_ada, b_ada, norm1_g, w_in, mlstm_conv, mlstm_gate_b, mlstm_norm_g, hgrn_conv, hgrn_f_b,
           hgrn_lb_raw, hgrn_norm_g, w_out, norm2_g, router_w, router_b, w_gu, b_gu, w_down, b_down, final_g):
    n_batch, seq, d = x.shape
    ctx_len = ctx.shape[1]
    depth = w_ada.shape[0]
    assert ctx_len == TILE and seq % TILE == 0 and n_batch + 1 <= 8
    tpb = (ctx_len + seq) // TILE
    ncc = ctx_len // CHUNK
    n = n_batch * (ctx_len + seq)
    n_blocks = -(-(n * TOP_K) // MOE_BLK) + N_EXPERTS
    n_rows = n_blocks * MOE_BLK

    xa = jnp.concatenate([ctx, x], axis=1).reshape(n, d)
    cond = jnp.zeros((8, d), F32).at[:n_batch].set(c).at[n_batch].set(c_ctx)
    mod = _ada_call(cond, w_ada, b_ada)

    lb_w = jax.nn.softmax(hgrn_lb_raw.astype(F32), axis=0)
    lower = jnp.cumsum(lb_w, axis=0) - lb_w[0]

    n_gate = 4 * ML_HEADS
    g0 = 3 * 512
    w_in_p = jnp.concatenate([w_in[:, :, :g0], jnp.pad(w_in[:, :, g0:g0 + n_gate], ((0, 0), (0, 0), (0, LANES - n_gate))),
                              w_in[:, :, g0 + n_gate:]], axis=2).astype(BF16)
    w_out_b = w_out.astype(BF16)
    w_gu_b = w_gu.astype(BF16)
    w_down_b = w_down.astype(BF16)

    for l in range(depth):
        last = l == depth - 1
        qk, v, o, gates, qi, ff, fb, go = _inproj_call(xa, mod[l], norm1_g[l], w_in_p[l], tpb, n_batch)
        q, k, hq, hv = _conv_call(qk, qi, mlstm_conv[l], hgrn_conv[l], tpb)
        mlf, mlb = _mlstm_call(q, k, v, gates, mlstm_gate_b[l], n_batch, ncc)
        hgf, hgb = _hgrn_call(hq, hv, ff, fb, hgrn_f_b[l], lower[l], n_batch, ncc)
        xa, h2, idx, gate = _out_call(xa, mlf, mlb, hgf, hgb, o, go, mod[l], mlstm_norm_g[l], hgrn_norm_g[l], w_out_b[l],
                                      norm2_g[l], router_w[l], router_b[l], tpb, n_batch)
        rank, cnt = _rank_call(idx)
        dest, meta = _dest_call(idx, rank, cnt, n_blocks)
        dest_flat = dest[:, :TOP_K].reshape(-1)
        xb = _scatter_call(dest_flat, h2, n_rows)
        yb = _expert_call(meta[:n_blocks, 0], meta[0:1, 1], xb, w_gu_b[l], b_gu[l], w_down_b[l], b_down[l])
        xa = _combine_call(dest_flat, xa, gate, mod[l], final_g, yb, tpb, n_batch, last)
    return xa.reshape(n_batch, seq, d)
```

```python
import functools

import jax
import jax.numpy as jnp
from jax import lax
from jax.experimental import pallas as pl
from jax.experimental.pallas import tpu as pltpu

F32 = jnp.float32
BF16 = jnp.bfloat16
HIGHEST = lax.Precision.HIGHEST

GRID_W = 64
CHUNK = 64
ML_HEADS = 4
ML_QK = 64
ML_V = 128
HG_HEADS = 4
HG_DK = 128
N_EXPERTS = 32
TOP_K = 4
SWIGLU_LIMIT = 7.0
SWIGLU_ALPHA = 1.702
EPS = 1e-6

TILE = 256
MOE_BLK = 256
LANES = 128
SUB = 16
NEG = -1e30
VMEM_LIMIT = 48 * 1024 * 1024
EXPERT_VMEM_LIMIT = 56 * 1024 * 1024


def _cparams(n_axes, vmem=VMEM_LIMIT):
    return pltpu.CompilerParams(dimension_semantics=("arbitrary",) * n_axes, vmem_limit_bytes=vmem)


def _dot(a, b):
    return jnp.dot(a, b, preferred_element_type=F32)


def _dot_nt(a, b):
    return lax.dot_general(a, b, (((1,), (1,)), ((), ())), preferred_element_type=F32)


def _dot_tn(a, b):
    return lax.dot_general(a, b, (((0,), (0,)), ((), ())), preferred_element_type=F32)


def _dot_hi(a, b):
    return jnp.dot(a, b, precision=HIGHEST, preferred_element_type=F32)


def _rms(x, g):
    return x * lax.rsqrt(jnp.mean(x * x, axis=-1, keepdims=True) + EPS) * g


def _sigmoid(x):
    return 1.0 / (1.0 + jnp.exp(-x))


def _log_sigmoid(x):
    return jnp.minimum(x, 0.0) - jnp.log(1.0 + jnp.exp(-jnp.abs(x)))


def _ada_kernel(cond_ref, w_ref, b_ref, o_ref):
    c = cond_ref[...]
    o_ref[...] = _dot_hi(c * _sigmoid(c), w_ref[...]) + b_ref[...]


def _ada_call(cond, w_ada, b_ada):
    depth, d, n6 = w_ada.shape
    tn = 1024
    return pl.pallas_call(
        _ada_kernel,
        grid=(depth, n6 // tn),
        in_specs=[
            pl.BlockSpec((8, d), lambda l, j: (0, 0)),
            pl.BlockSpec((None, d, tn), lambda l, j: (l, 0, j)),
            pl.BlockSpec((None, 1, tn), lambda l, j: (l, 0, j)),
        ],
        out_specs=pl.BlockSpec((None, 8, tn), lambda l, j: (l, 0, j)),
        out_shape=jax.ShapeDtypeStruct((depth, 8, n6), F32),
        compiler_params=_cparams(2),
        name="ada",
    )(cond, w_ada, b_ada.reshape(depth, 1, n6))


def _mod_row(mod_ref, tpb, n_batch):
    i = pl.program_id(0)
    row = jnp.where(i % tpb == 0, n_batch, i // tpb)
    return mod_ref[pl.ds(row, 1), :]


IN_SEGS = (512, 512, 512, 128, 1024, 512, 512, 512)


def _inproj_kernel(x_ref, mod_ref, g_ref, w_ref, *outs, tpb, n_batch):
    d = x_ref.shape[1]
    mod = _mod_row(mod_ref, tpb, n_batch)
    h = _rms(x_ref[...], g_ref[...]) * (1.0 + mod[:, d:2 * d]) + mod[:, 0:d]
    hb = h.astype(BF16)
    off = 0
    for o_ref, n in zip(outs, IN_SEGS):
        o_ref[...] = _dot(hb, w_ref[:, off:off + n])
        off += n


def _inproj_call(x, mod_l, g, w_p, tpb, n_batch):
    n, d = x.shape
    nw = w_p.shape[1]
    return pl.pallas_call(
        functools.partial(_inproj_kernel, tpb=tpb, n_batch=n_batch),
        grid=(n // TILE,),
        in_specs=[
            pl.BlockSpec((TILE, d), lambda i: (i, 0)),
            pl.BlockSpec((8, 6 * d), lambda i: (0, 0)),
            pl.BlockSpec((1, d), lambda i: (0, 0)),
            pl.BlockSpec((d, nw), lambda i: (0, 0)),
        ],
        out_specs=[pl.BlockSpec((TILE, s), lambda i: (i, 0)) for s in IN_SEGS],
        out_shape=[jax.ShapeDtypeStruct((n, s), F32) for s in IN_SEGS],
        compiler_params=_cparams(1),
        name="inproj",
    )(x, mod_l, g.reshape(1, d), w_p)


def _conv_kernel(qk_m, qk_p, qk_n, qi_m, qi_p, qi_n, wq_ref, wi_ref, q_ref, k_ref, hq_ref, hv_ref, *, tpb):
    r = lax.broadcasted_iota(jnp.int32, (TILE, 1), 0)
    j = jnp.zeros((TILE, 1), jnp.int32) + pl.program_id(0) % tpb
    is_ctx = j == 0
    col = r % GRID_W
    ok_l = jnp.logical_or(col != 0, jnp.logical_and(is_ctx, r != 0))
    ok_r = jnp.logical_or(col != GRID_W - 1, jnp.logical_and(is_ctx, r != TILE - 1))
    ok_u = jnp.logical_and(jnp.logical_not(is_ctx), jnp.logical_or(j != 1, r >= GRID_W))
    ok_d = jnp.logical_and(jnp.logical_not(is_ctx), jnp.logical_or(j != tpb - 1, r < TILE - GRID_W))
    col_ok = (ok_l, None, ok_r)
    row_ok = (ok_u, None, ok_d)
    n_ext = TILE + 2 * GRID_W

    def conv(main, prev, nxt, w_ref, c0, c1):
        ext = jnp.concatenate([prev[:, c0:c1], main[:, c0:c1], nxt[:, c0:c1]], axis=0)
        shifted = (pltpu.roll(ext, 1, 0), ext, pltpu.roll(ext, n_ext - 1, 0))
        acc = jnp.zeros((TILE, c1 - c0), F32)
        for kh in range(3):
            for kw in range(3):
                tap = shifted[kw][kh * GRID_W:kh * GRID_W + TILE]
                ok = None
                for m in (row_ok[kh], col_ok[kw]):
                    if m is not None:
                        ok = m if ok is None else jnp.logical_and(ok, m)
                if ok is not None:
                    tap = jnp.where(ok, tap, 0.0)
                acc = acc + tap * w_ref[kh * 3 + kw:kh * 3 + kw + 1, c0:c1]
        return acc * _sigmoid(acc)

    nq = ML_HEADS * ML_QK
    q_ref[...] = conv(qk_m, qk_p, qk_n, wq_ref, 0, nq) * (ML_QK ** -0.5)
    k_ref[...] = conv(qk_m, qk_p, qk_n, wq_ref, nq, 2 * nq)
    nh = HG_HEADS * HG_DK
    hq_ref[...] = conv(qi_m, qi_p, qi_n, wi_ref, 0, nh)
    hv_ref[...] = conv(qi_m, qi_p, qi_n, wi_ref, nh, 2 * nh)


def _conv_call(qk, qi, wq, wi, tpb):
    n = qk.shape[0]
    rpt = TILE // GRID_W
    nrow = n // GRID_W
    cq, ci = qk.shape[1], qi.shape[1]

    def main(i):
        return (i, 0)

    def prev(i):
        return (jnp.maximum(i * rpt - 1, 0), 0)

    def nxt(i):
        return (jnp.minimum((i + 1) * rpt, nrow - 1), 0)

    return pl.pallas_call(
        functools.partial(_conv_kernel, tpb=tpb),
        grid=(n // TILE,),
        in_specs=[
            pl.BlockSpec((TILE, cq), main), pl.BlockSpec((GRID_W, cq), prev), pl.BlockSpec((GRID_W, cq), nxt),
            pl.BlockSpec((TILE, ci), main), pl.BlockSpec((GRID_W, ci), prev), pl.BlockSpec((GRID_W, ci), nxt),
            pl.BlockSpec((9, cq), lambda i: (0, 0)),
            pl.BlockSpec((9, ci), lambda i: (0, 0)),
        ],
        out_specs=[pl.BlockSpec((TILE, s), main) for s in (cq // 2, cq // 2, ci // 2, ci // 2)],
        out_shape=[jax.ShapeDtypeStruct((n, s), F32) for s in (cq // 2, cq // 2, ci // 2, ci // 2)],
        compiler_params=_cparams(1),
        name="conv",
    )(qk, qk, qk, qi, qi, qi, wq.reshape(9, cq), wi.reshape(9, ci))


W_CHUNKS = TILE // CHUNK


def _tile_maps(nct, nt):
    def fwd(b, j):
        return (b * nt + j, 0)

    def bwd(b, j):
        return (b * nt + jnp.where(j < nct, nct - 1 - j, nt + nct - 1 - j), 0)

    return fwd, bwd


def _tri(fwd):
    t = lax.broadcasted_iota(jnp.int32, (CHUNK, CHUNK), 0)
    s = lax.broadcasted_iota(jnp.int32, (CHUNK, CHUNK), 1)
    return (s <= t) if fwd else (s >= t)


def _mask_dot(mask_bf, x):
    a = x.astype(BF16)
    r = x - a.astype(F32)
    b = r.astype(BF16)
    c = (r - b.astype(F32)).astype(BF16)
    return (_dot(mask_bf, a) + _dot(mask_bf, b)) + _dot(mask_bf, c)


def _mlstm_kernel(qf, kf, vf, gf, qb, kb, vb, gb, bias_ref, hf_ref, hb_ref, ct_ref, m_ref):
    @pl.when(pl.program_id(1) == 0)
    def _():
        ct_ref[...] = jnp.zeros_like(ct_ref)
        m_ref[...] = jnp.zeros_like(m_ref)

    lane = lax.broadcasted_iota(jnp.int32, (CHUNK, LANES), 1)
    ones_col = jnp.where(lane == 0, 1.0, 0.0).astype(BF16)
    for d, (q_ref, k_ref, v_ref, g_ref, h_ref) in enumerate(((qf, kf, vf, gf, hf_ref), (qb, kb, vb, gb, hb_ref))):
        fwd = d == 0
        mask = _tri(fwd)
        mask_bf = jnp.where(mask, 1.0, 0.0).astype(BF16)
        last = CHUNK - 1 if fwd else 0
        order = range(W_CHUNKS) if fwd else range(W_CHUNKS - 1, -1, -1)
        local = {}
        for c in order:
            rows = slice(c * CHUNK, (c + 1) * CHUNK)
            g_all = g_ref[rows, :] + bias_ref[...]
            bcum = _mask_dot(mask_bf, _log_sigmoid(g_all))
            g_t = g_all.T
            b_t = bcum.T
            for h in range(ML_HEADS):
                ci = d * ML_HEADS + h
                cf = (2 + d) * ML_HEADS + h
                b_col, b_row = bcum[:, cf:cf + 1], b_t[cf:cf + 1, :]
                i_col, i_row = g_all[:, ci:ci + 1], g_t[ci:ci + 1, :]
                dmat = jnp.where(mask, b_col - b_row + i_row, -jnp.inf)
                m_loc = jnp.max(dmat, axis=-1, keepdims=True)
                k = k_ref[rows, h * ML_QK:(h + 1) * ML_QK]
                qbf = q_ref[rows, h * ML_QK:(h + 1) * ML_QK].astype(BF16)
                s = _dot_nt(qbf, k.astype(BF16)) * jnp.exp(dmat - m_loc)
                vaug = jnp.concatenate([v_ref[rows, h * ML_V:(h + 1) * ML_V].astype(BF16), ones_col], axis=1)
                nd_loc = _dot(s.astype(BF16), vaug)
                g_tot = b_col[last:last + 1, :]
                m_e = jnp.max(g_tot - b_row + i_row, axis=-1, keepdims=True)
                kw = (k * jnp.exp(g_tot - b_col + i_col - m_e)).astype(BF16)
                local[(c, h)] = (b_col, m_loc, nd_loc, g_tot, m_e, _dot_tn(kw, vaug), qbf)
        for h in range(ML_HEADS):
            ci = d * ML_HEADS + h
            m = m_ref[ci:ci + 1, 0:1]
            ct = ct_ref[ci]
            for c in order:
                b_col, m_loc, nd_loc, g_tot, m_e, upd, qbf = local[(c, h)]
                a_inter = b_col + m
                m_t = jnp.maximum(a_inter, m_loc)
                nd = jnp.exp(m_loc - m_t) * nd_loc + jnp.exp(a_inter - m_t) * _dot(qbf, ct.astype(BF16))
                den = nd[:, ML_V:ML_V + 1]
                h_ref[c * CHUNK:(c + 1) * CHUNK, h * ML_V:(h + 1) * ML_V] = (
                    nd[:, 0:ML_V] / jnp.maximum(jnp.abs(den), jnp.exp(-m_t)))
                m_new = jnp.maximum(g_tot + m, m_e)
                ct = jnp.exp(g_tot + m - m_new) * ct + jnp.exp(m_e - m_new) * upd
                m = m_new
            ct_ref[ci] = ct
            m_ref[ci:ci + 1, :] = jnp.broadcast_to(m, (1, LANES))


def _mlstm_call(q, k, v, gates, gate_b, n_batch, nct):
    n = q.shape[0]
    nt = n // TILE // n_batch
    fwd, bwd = _tile_maps(nct, nt)
    dq, dv = q.shape[1], v.shape[1]
    specs = lambda m: [pl.BlockSpec((TILE, dq), m), pl.BlockSpec((TILE, dq), m),
                       pl.BlockSpec((TILE, dv), m), pl.BlockSpec((TILE, LANES), m)]
    bias = jnp.zeros((1, LANES), F32).at[0, :4 * ML_HEADS].set(gate_b.reshape(-1))
    return pl.pallas_call(
        _mlstm_kernel,
        grid=(n_batch, nt),
        in_specs=specs(fwd) + specs(bwd) + [pl.BlockSpec((1, LANES), lambda b, j: (0, 0))],
        out_specs=[pl.BlockSpec((TILE, dv), fwd), pl.BlockSpec((TILE, dv), bwd)],
        out_shape=[jax.ShapeDtypeStruct((n, dv), F32)] * 2,
        scratch_shapes=[pltpu.VMEM((2 * ML_HEADS, ML_QK, ML_V + LANES), F32), pltpu.VMEM((2 * ML_HEADS, LANES), F32)],
        compiler_params=_cparams(2),
        name="mlstm",
    )(q, k, v, gates, q, k, v, gates, bias)


def _hgrn_kernel(qf, vf, ff, qb, vb, fb, fbias_ref, lb_ref, of_ref, ob_ref, st_ref):
    @pl.when(pl.program_id(1) == 0)
    def _():
        st_ref[...] = jnp.zeros_like(st_ref)

    n_sub = CHUNK // SUB
    rows = lax.broadcasted_iota(jnp.int32, (CHUNK, 1), 0)
    t_i = lax.broadcasted_iota(jnp.int32, (CHUNK, CHUNK), 0)
    s_i = lax.broadcasted_iota(jnp.int32, (CHUNK, CHUNK), 1)
    same_sub = (t_i // SUB) == (s_i // SUB)
    lb = lb_ref[...]
    for d, (q_ref, v_ref, f_ref, o_ref) in enumerate(((qf, vf, ff, of_ref), (qb, vb, fb, ob_ref))):
        fwd = d == 0
        mask = _tri(fwd)
        mask_bf = jnp.where(mask, 1.0, 0.0).astype(BF16)
        diag_ok = jnp.logical_and(same_sub, mask)
        last = CHUNK - 1 if fwd else 0
        order = range(W_CHUNKS) if fwd else range(W_CHUNKS - 1, -1, -1)
        local = {}
        for c in order:
            rs = slice(c * CHUNK, (c + 1) * CHUNK)
            f = lb + (1.0 - lb) * _sigmoid(f_ref[rs, :] + fbias_ref[d:d + 1, :])
            kk = 1.0 - f
            b = _mask_dot(mask_bf, jnp.log(f))
            q = q_ref[rs, :]
            q_parts, k_parts = [], []
            for p in range(1, n_sub):
                if fwd:
                    ref_row, q_ok, k_ok = p * SUB - 1, (rows // SUB) == p, rows < p * SUB
                else:
                    ref_row, q_ok, k_ok = p * SUB, (rows // SUB) == p - 1, rows >= p * SUB
                r_p = b[ref_row:ref_row + 1, :]
                q_parts.append(jnp.where(q_ok, q * jnp.exp(jnp.minimum(b - r_p, 0.0)), 0.0))
                k_parts.append(jnp.where(k_ok, kk * jnp.exp(jnp.minimum(r_p - b, 0.0)), 0.0))
            mid = SUB // 2 - 1 if fwd else SUB // 2
            c_m = b[mid:mid + 1, :]
            for i in range(1, n_sub):
                c_m = jnp.where(rows >= i * SUB, b[i * SUB + mid:i * SUB + mid + 1, :], c_m)
            q_d = q * jnp.exp(b - c_m)
            k_d = kk * jnp.exp(c_m - b)
            g_row = b[last:last + 1, :]
            q_in = (q * jnp.exp(b)).astype(BF16)
            k_out = (kk * jnp.exp(g_row - b)).astype(BF16)
            decay = jnp.exp(g_row)
            for h in range(HG_HEADS):
                sl = slice(h * HG_DK, (h + 1) * HG_DK)
                qc = jnp.concatenate([x[:, sl] for x in q_parts], axis=1).astype(BF16)
                kc = jnp.concatenate([x[:, sl] for x in k_parts], axis=1).astype(BF16)
                a = _dot_nt(qc, kc) + jnp.where(diag_ok, _dot_nt(q_d[:, sl].astype(BF16), k_d[:, sl].astype(BF16)), 0.0)
                vh = v_ref[rs, sl].astype(BF16)
                local[(c, h)] = (_dot(a.astype(BF16), vh), q_in[:, sl], decay[:, sl], _dot_tn(vh, k_out[:, sl]))
        for h in range(HG_HEADS):
            st = st_ref[d * HG_HEADS + h]
            for c in order:
                o_loc, q_in, decay, upd = local[(c, h)]
                o_ref[c * CHUNK:(c + 1) * CHUNK, h * HG_DK:(h + 1) * HG_DK] = o_loc + _dot_nt(q_in, st.astype(BF16))
                st = st * decay + upd
            st_ref[d * HG_HEADS + h] = st


def _hgrn_call(hq, hv, ff, fb, f_b, lb, n_batch, nct):
    n, w = hq.shape
    nt = n // TILE // n_batch
    fwd, bwd = _tile_maps(nct, nt)
    spec = lambda m: pl.BlockSpec((TILE, w), m)
    return pl.pallas_call(
        _hgrn_kernel,
        grid=(n_batch, nt),
        in_specs=[spec(fwd)] * 3 + [spec(bwd)] * 3 + [pl.BlockSpec((2, w), lambda b, j: (0, 0)),
                                                      pl.BlockSpec((1, w), lambda b, j: (0, 0))],
        out_specs=[spec(fwd), spec(bwd)],
        out_shape=[jax.ShapeDtypeStruct((n, w), F32)] * 2,
        scratch_shapes=[pltpu.VMEM((2 * HG_HEADS, HG_DK, HG_DK), F32)],
        compiler_params=_cparams(2),
        name="hgrn",
    )(hq, hv, ff, hq, hv, fb, f_b, lb.reshape(1, w))


def _head_rms(h, g, width):
    parts = []
    for i in range(h.shape[1] // width):
        p = h[:, i * width:(i + 1) * width]
        parts.append(p * lax.rsqrt(jnp.mean(p * p, axis=-1, keepdims=True) + EPS))
    return jnp.concatenate(parts, axis=1) * g


def _out_kernel(x_ref, mlf, mlb, hgf, hgb, o_ref, go_ref, mod_ref, gml_ref, ghg_ref, wout_ref, g2_ref, rw_ref, rb_ref,
                xo_ref, h2_ref, idx_ref, gate_ref, *, tpb, n_batch):
    d = x_ref.shape[1]
    mod = _mod_row(mod_ref, tpb, n_batch)
    y_ml = _sigmoid(o_ref[...]) * _head_rms(mlf[...] + mlb[...], gml_ref[...], ML_V)
    go = go_ref[...]
    y_hg = go * _sigmoid(go) * _head_rms(hgf[...] + hgb[...], ghg_ref[...], HG_DK)
    y = jnp.concatenate([y_ml, y_hg], axis=1).astype(BF16)
    x = x_ref[...] + mod[:, 2 * d:3 * d] * _dot(y, wout_ref[...])
    xo_ref[...] = x
    h2 = _rms(x, g2_ref[...]) * (1.0 + mod[:, 4 * d:5 * d]) + mod[:, 3 * d:4 * d]
    h2_ref[...] = h2
    vals = _dot_hi(h2, rw_ref[...]) + rb_ref[...]
    lane = lax.broadcasted_iota(jnp.int32, vals.shape, 1)
    lane_f = lane.astype(F32)
    idx_out = jnp.zeros(vals.shape, F32)
    top = []
    for k in range(TOP_K):
        mx = jnp.max(vals, axis=-1, keepdims=True)
        ix = jnp.min(jnp.where(vals == mx, lane_f, float(LANES)), axis=-1, keepdims=True)
        top.append(mx)
        idx_out = jnp.where(lane == k, ix, idx_out)
        vals = jnp.where(lane_f == ix, -jnp.inf, vals)
    ex = [jnp.exp(t - top[0]) for t in top]
    tot = ex[0] + ex[1] + ex[2] + ex[3]
    gate_out = jnp.zeros(vals.shape, F32)
    for k in range(TOP_K):
        gate_out = jnp.where(lane == k, ex[k] / tot, gate_out)
    idx_ref[...] = idx_out.astype(jnp.int32)
    gate_ref[...] = gate_out


def _out_call(x, mlf, mlb, hgf, hgb, o, go, mod_l, gml, ghg, wout, g2, rw, rb, tpb, n_batch):
    n, d = x.shape
    w = mlf.shape[1]
    row = lambda i: (i, 0)
    fix = lambda i: (0, 0)
    rw_p = jnp.zeros((d, LANES), F32).at[:, :N_EXPERTS].set(rw)
    rb_p = jnp.full((1, LANES), NEG, F32).at[0, :N_EXPERTS].set(rb)
    return pl.pallas_call(
        functools.partial(_out_kernel, tpb=tpb, n_batch=n_batch),
        grid=(n // TILE,),
        in_specs=[pl.BlockSpec((TILE, d), row)] + [pl.BlockSpec((TILE, w), row)] * 6 + [
            pl.BlockSpec((8, 6 * d), fix), pl.BlockSpec((1, w), fix), pl.BlockSpec((1, w), fix),
            pl.BlockSpec((d, d), fix), pl.BlockSpec((1, d), fix), pl.BlockSpec((d, LANES), fix),
            pl.BlockSpec((1, LANES), fix)],
        out_specs=[pl.BlockSpec((TILE, d), row), pl.BlockSpec((TILE, d), row),
                   pl.BlockSpec((TILE, LANES), row), pl.BlockSpec((TILE, LANES), row)],
        out_shape=[jax.ShapeDtypeStruct((n, d), F32), jax.ShapeDtypeStruct((n, d), F32),
                   jax.ShapeDtypeStruct((n, LANES), jnp.int32), jax.ShapeDtypeStruct((n, LANES), F32)],
        compiler_params=_cparams(1),
        name="outproj",
    )(x, mlf, mlb, hgf, hgb, o, go, mod_l, gml.reshape(1, w), ghg.reshape(1, w), wout, g2.reshape(1, d), rw_p, rb_p)


def _onehots(idx):
    lane = lax.broadcasted_iota(jnp.int32, idx.shape, 1)
    return [lane == idx[:, k:k + 1] for k in range(TOP_K)]


def _rank_kernel(idx_ref, rank_ref, cnt_ref, carry_ref):
    @pl.when(pl.program_id(0) == 0)
    def _():
        carry_ref[...] = jnp.zeros_like(carry_ref)

    hots = _onehots(idx_ref[...])
    m = jnp.zeros(idx_ref.shape, F32)
    for hk in hots:
        m = m + jnp.where(hk, 1.0, 0.0)
    t_i = lax.broadcasted_iota(jnp.int32, (TILE, TILE), 0)
    s_i = lax.broadcasted_iota(jnp.int32, (TILE, TILE), 1)
    before = _dot((s_i < t_i).astype(BF16), m.astype(BF16)) + carry_ref[...]
    lane = lax.broadcasted_iota(jnp.int32, idx_ref.shape, 1)
    out = jnp.zeros(idx_ref.shape, F32)
    for k, hk in enumerate(hots):
        rk = jnp.sum(jnp.where(hk, before, 0.0), axis=-1, keepdims=True)
        out = jnp.where(lane == k, rk, out)
    rank_ref[...] = out.astype(jnp.int32)
    carry_ref[...] = carry_ref[...] + jnp.sum(m, axis=0, keepdims=True)
    cnt_ref[...] = carry_ref[...]


def _rank_call(idx):
    n = idx.shape[0]
    return pl.pallas_call(
        _rank_kernel,
        grid=(n // TILE,),
        in_specs=[pl.BlockSpec((TILE, LANES), lambda i: (i, 0))],
        out_specs=[pl.BlockSpec((TILE, LANES), lambda i: (i, 0)), pl.BlockSpec((1, LANES), lambda i: (0, 0))],
        out_shape=[jax.ShapeDtypeStruct((n, LANES), jnp.int32), jax.ShapeDtypeStruct((1, LANES), F32)],
        scratch_shapes=[pltpu.VMEM((1, LANES), F32)],
        compiler_params=_cparams(1),
        name="rank",
    )(idx)


def _dest_kernel(idx_ref, rank_ref, cnt_ref, dest_ref, meta_ref, *, n_blocks):
    cnt = cnt_ref[...]
    padded = jnp.floor((cnt + (MOE_BLK - 1)) * (1.0 / MOE_BLK)) * MOE_BLK
    e_i = lax.broadcasted_iota(jnp.int32, (LANES, LANES), 0)
    e_j = lax.broadcasted_iota(jnp.int32, (LANES, LANES), 1)
    pad_start = _dot_hi(jnp.broadcast_to(padded, (8, LANES)), (e_i < e_j).astype(F32))[0:1]
    pad_end = pad_start + padded
    lane = lax.broadcasted_iota(jnp.int32, idx_ref.shape, 1)
    rank = rank_ref[...]
    out = jnp.zeros(idx_ref.shape, jnp.int32)
    for k, hk in enumerate(_onehots(idx_ref[...])):
        st = jnp.sum(jnp.where(hk, pad_start, 0.0), axis=-1, keepdims=True).astype(jnp.int32)
        out = jnp.where(lane == k, st + rank[:, k:k + 1], out)
    dest_ref[...] = out
    blk = lax.broadcasted_iota(jnp.int32, meta_ref.shape, 0).astype(F32) * MOE_BLK
    lane_m = lax.broadcasted_iota(jnp.int32, meta_ref.shape, 1)
    done = jnp.where(jnp.logical_and(lane_m < N_EXPERTS, pad_end <= blk), 1.0, 0.0)
    be = jnp.minimum(jnp.sum(done, axis=-1, keepdims=True), N_EXPERTS - 1.0)
    used = jnp.sum(jnp.where(lane_m[0:1] < N_EXPERTS, padded, 0.0), axis=-1, keepdims=True) * (1.0 / MOE_BLK)
    meta_ref[...] = jnp.where(lane_m == 0, be, jnp.where(lane_m == 1, used, 0.0)).astype(jnp.int32)


def _dest_call(idx, rank, cnt, n_blocks):
    n = idx.shape[0]
    nb_pad = -(-n_blocks // 8) * 8
    row = lambda i: (i, 0)
    return pl.pallas_call(
        functools.partial(_dest_kernel, n_blocks=n_blocks),
        grid=(n // TILE,),
        in_specs=[pl.BlockSpec((TILE, LANES), row), pl.BlockSpec((TILE, LANES), row),
                  pl.BlockSpec((1, LANES), lambda i: (0, 0))],
        out_specs=[pl.BlockSpec((TILE, LANES), row), pl.BlockSpec((nb_pad, LANES), lambda i: (0, 0))],
        out_shape=[jax.ShapeDtypeStruct((n, LANES), jnp.int32), jax.ShapeDtypeStruct((nb_pad, LANES), jnp.int32)],
        compiler_params=_cparams(1),
        name="dest",
    )(idx, rank, cnt)


ROW_UNROLL = 8


def _scatter_kernel(dest_ref, h_ref, xb_in, xb_ref, sem):
    del xb_in

    def issue(g, c):
        for u in range(ROW_UNROLL):
            r = g * ROW_UNROLL + u
            for k in range(TOP_K):
                dst = dest_ref[g * (ROW_UNROLL * TOP_K) + u * TOP_K + k]
                pltpu.make_async_copy(h_ref.at[pl.ds(r, 1), :], xb_ref.at[pl.ds(dst, 1), :], sem).start(priority=k % 2)
        return c

    lax.fori_loop(0, TILE // ROW_UNROLL, issue, 0)
    all_rows = xb_ref.at[pl.ds(0, TILE * TOP_K), :]
    pltpu.make_async_copy(all_rows, all_rows, sem).wait()


def _scatter_call(dest_flat, h2, n_rows):
    n, d = h2.shape
    xb0 = jnp.zeros((n_rows, d), F32)
    return pl.pallas_call(
        _scatter_kernel,
        grid=(n // TILE,),
        in_specs=[pl.BlockSpec((TILE * TOP_K,), lambda i: (i,), memory_space=pltpu.SMEM),
                  pl.BlockSpec((TILE, d), lambda i: (i, 0)),
                  pl.BlockSpec(memory_space=pl.ANY)],
        out_specs=pl.BlockSpec(memory_space=pl.ANY),
        out_shape=jax.ShapeDtypeStruct((n_rows, d), F32),
        scratch_shapes=[pltpu.SemaphoreType.DMA(())],
        input_output_aliases={2: 0},
        compiler_params=_cparams(1),
        name="scatter",
    )(dest_flat, h2, xb0)


CAST_ROWS = 64


def _cast_rows(src_ref, dst_ref):
    def body(r, c):
        rows = pl.ds(pl.multiple_of(r * CAST_ROWS, CAST_ROWS), CAST_ROWS)
        dst_ref[rows, :] = src_ref[rows, :].astype(BF16)
        return c

    lax.fori_loop(0, src_ref.shape[0] // CAST_ROWS, body, 0)


def _expert_kernel(be_ref, used_ref, x_ref, wgu_ref, bgu_ref, wd_ref, bd_ref, y_ref, wgu_bf, wd_bf):
    i = pl.program_id(0)
    live = i < used_ref[0]
    new_expert = jnp.logical_or(i == 0, be_ref[i] != be_ref[jnp.maximum(i - 1, 0)])

    @pl.when(jnp.logical_not(live))
    def _():
        y_ref[...] = jnp.zeros_like(y_ref)

    @pl.when(jnp.logical_and(live, new_expert))
    def _():
        _cast_rows(wgu_ref, wgu_bf)
        _cast_rows(wd_ref, wd_bf)

    @pl.when(live)
    def _():
        de = wd_ref.shape[0]
        gu = _dot(x_ref[...].astype(BF16), wgu_bf[...]) + bgu_ref[...]
        glu = jnp.minimum(gu[:, :de], SWIGLU_LIMIT)
        lin = jnp.clip(gu[:, de:], -SWIGLU_LIMIT, SWIGLU_LIMIT)
        act = glu * _sigmoid(SWIGLU_ALPHA * glu) * (lin + 1.0)
        y_ref[...] = _dot(act.astype(BF16), wd_bf[...]) + bd_ref[...]


def _expert_call(be, used, xb, wgu, bgu, wd, bd):
    n_rows, d = xb.shape
    e, _, de2 = wgu.shape
    nb = n_rows // MOE_BLK
    blk = lambda i, be, used: (jnp.minimum(i, used[0] - 1), 0)
    wsel = lambda i, be, used: (be[jnp.minimum(i, used[0] - 1)], 0, 0)
    return pl.pallas_call(
        _expert_kernel,
        grid_spec=pltpu.PrefetchScalarGridSpec(
            num_scalar_prefetch=2,
            grid=(nb,),
            in_specs=[pl.BlockSpec((MOE_BLK, d), blk),
                      pl.BlockSpec((None, d, de2), wsel), pl.BlockSpec((None, 1, de2), wsel),
                      pl.BlockSpec((None, de2 // 2, d), wsel), pl.BlockSpec((None, 1, d), wsel)],
            out_specs=pl.BlockSpec((MOE_BLK, d), lambda i, be, used: (i, 0)),
            scratch_shapes=[pltpu.VMEM((d, de2), BF16), pltpu.VMEM((de2 // 2, d), BF16)],
        ),
        out_shape=jax.ShapeDtypeStruct((n_rows, d), F32),
        compiler_params=_cparams(1, vmem=EXPERT_VMEM_LIMIT),
        name="expert",
    )(be, used, xb, wgu, bgu.reshape(e, 1, de2), wd, bd.reshape(e, 1, d))


def _combine_kernel(dest_ref, x_ref, gate_ref, mod_ref, fg_ref, yb_ref, o_ref, buf, sem, *, tpb, n_batch, final):
    def issue(g, c):
        for u in range(ROW_UNROLL):
            r = g * ROW_UNROLL + u
            for k in range(TOP_K):
                src = dest_ref[g * (ROW_UNROLL * TOP_K) + u * TOP_K + k]
                pltpu.make_async_copy(yb_ref.at[pl.ds(src, 1), :], buf.at[k, pl.ds(r, 1), :], sem).start(priority=k % 2)
        return c

    lax.fori_loop(0, TILE // ROW_UNROLL, issue, 0)
    pltpu.make_async_copy(buf, buf, sem).wait()
    d = x_ref.shape[1]
    mod = _mod_row(mod_ref, tpb, n_batch)
    gates = gate_ref[...]
    f = gates[:, 0:1] * buf[0]
    for k in range(1, TOP_K):
        f = f + gates[:, k:k + 1] * buf[k]
    x = x_ref[...] + mod[:, 5 * d:6 * d] * f
    o_ref[...] = _rms(x, fg_ref[...]) if final else x


def _combine_call(dest_flat, x, gates, mod_l, fg, yb, tpb, n_batch, final):
    n, d = x.shape
    if final:
        out_rows = n - n_batch * TILE
        omap = lambda i: ((i // tpb) * (tpb - 1) + jnp.maximum(i % tpb - 1, 0), 0)
    else:
        out_rows = n
        omap = lambda i: (i, 0)
    return pl.pallas_call(
        functools.partial(_combine_kernel, tpb=tpb, n_batch=n_batch, final=final),
        grid=(n // TILE,),
        in_specs=[pl.BlockSpec((TILE * TOP_K,), lambda i: (i,), memory_space=pltpu.SMEM),
                  pl.BlockSpec((TILE, d), lambda i: (i, 0)),
                  pl.BlockSpec((TILE, LANES), lambda i: (i, 0)),
                  pl.BlockSpec((8, 6 * d), lambda i: (0, 0)),
                  pl.BlockSpec((1, d), lambda i: (0, 0)),
                  pl.BlockSpec(memory_space=pl.ANY)],
        out_specs=pl.BlockSpec((TILE, d), omap),
        out_shape=jax.ShapeDtypeStruct((out_rows, d), F32),
        scratch_shapes=[pltpu.VMEM((TOP_K, TILE, d), F32), pltpu.SemaphoreType.DMA(())],
        compiler_params=_cparams(1),
        name="combine",
    )(dest_flat, x, gates, mod_l, fg.reshape(1, d), yb)


def kernel(x, c, ctx, c_ctx, w_ada, b_ada, norm1_g, w_in, mlstm_conv, mlstm_gate_b, mlstm_norm_g, hgrn_conv, hgrn_f_b,
           hgrn_lb_raw, hgrn_norm_g, w_out, norm2_g, router_w, router_b, w_gu, b_gu, w_down, b_down, final_g):
    n_batch, seq, d = x.shape
    ctx_len = ctx.shape[1]
    depth = w_ada.shape[0]
    assert ctx_len == TILE and seq % TILE == 0 and n_batch + 1 <= 8
    tpb = (ctx_len + seq) // TILE
    nct = ctx_len // TILE
    n = n_batch * (ctx_len + seq)
    n_blocks = -(-(n * TOP_K) // MOE_BLK) + N_EXPERTS
    n_rows = n_blocks * MOE_BLK

    xa = jnp.concatenate([ctx, x], axis=1).reshape(n, d)
    cond = jnp.zeros((8, d), F32).at[:n_batch].set(c).at[n_batch].set(c_ctx)
    mod = _ada_call(cond, w_ada, b_ada)

    lb_w = jax.nn.softmax(hgrn_lb_raw.astype(F32), axis=0)
    lower = jnp.cumsum(lb_w, axis=0) - lb_w[0]

    n_gate = 4 * ML_HEADS
    g0 = 3 * 512
    w_in_p = jnp.concatenate([w_in[:, :, :g0], jnp.pad(w_in[:, :, g0:g0 + n_gate], ((0, 0), (0, 0), (0, LANES - n_gate))),
                              w_in[:, :, g0 + n_gate:]], axis=2).astype(BF16)
    w_out_b = w_out.astype(BF16)

    for l in range(depth):
        last = l == depth - 1
        qk, v, o, gates, qi, ff, fb, go = _inproj_call(xa, mod[l], norm1_g[l], w_in_p[l], tpb, n_batch)
        q, k, hq, hv = _conv_call(qk, qi, mlstm_conv[l], hgrn_conv[l], tpb)
        mlf, mlb = _mlstm_call(q, k, v, gates, mlstm_gate_b[l], n_batch, nct)
        hgf, hgb = _hgrn_call(hq, hv, ff, fb, hgrn_f_b[l], lower[l], n_batch, nct)
        xa, h2, idx, gate = _out_call(xa, mlf, mlb, hgf, hgb, o, go, mod[l], mlstm_norm_g[l], hgrn_norm_g[l], w_out_b[l],
                                      norm2_g[l], router_w[l], router_b[l], tpb, n_batch)
        rank, cnt = _rank_call(idx)
        dest, meta = _dest_call(idx, rank, cnt, n_blocks)
        dest_flat = dest[:, :TOP_K].reshape(-1)
        xb = _scatter_call(dest_flat, h2, n_rows)
        yb = _expert_call(meta[:n_blocks, 0], meta[0:1, 1], xb, w_gu[l], b_gu[l], w_down[l], b_down[l])
        xa = _combine_call(dest_flat, xa, gate, mod[l], final_g, yb, tpb, n_batch, last)
    return xa.reshape(n_batch, seq, d)
```

```python
import functools

import jax
import jax.numpy as jnp
from jax import lax
from jax.experimental import pallas as pl
from jax.experimental.pallas import tpu as pltpu

F32 = jnp.float32
BF16 = jnp.bfloat16
HIGHEST = lax.Precision.HIGHEST

GRID_W = 64
CHUNK = 64
ML_HEADS = 4
ML_QK = 64
ML_V = 128
HG_HEADS = 4
HG_DK = 128
N_EXPERTS = 32
TOP_K = 4
SWIGLU_LIMIT = 7.0
SWIGLU_ALPHA = 1.702
EPS = 1e-6

TILE = 256
MOE_BLK = 256
LANES = 128
SUB = 16
NEG = -1e30
VMEM_LIMIT = 48 * 1024 * 1024
EXPERT_VMEM_LIMIT = 56 * 1024 * 1024


def _cparams(n_axes, vmem=VMEM_LIMIT):
    return pltpu.CompilerParams(dimension_semantics=("arbitrary",) * n_axes, vmem_limit_bytes=vmem)


def _dot(a, b):
    return jnp.dot(a, b, preferred_element_type=F32)


def _dot_nt(a, b):
    return lax.dot_general(a, b, (((1,), (1,)), ((), ())), preferred_element_type=F32)


def _dot_tn(a, b):
    return lax.dot_general(a, b, (((0,), (0,)), ((), ())), preferred_element_type=F32)


def _dot_hi(a, b):
    return jnp.dot(a, b, precision=HIGHEST, preferred_element_type=F32)


def _rms(x, g):
    return x * lax.rsqrt(jnp.mean(x * x, axis=-1, keepdims=True) + EPS) * g


def _sigmoid(x):
    return 1.0 / (1.0 + jnp.exp(-x))


def _log_sigmoid(x):
    return jnp.minimum(x, 0.0) - jnp.log(1.0 + jnp.exp(-jnp.abs(x)))


def _ada_kernel(cond_ref, w_ref, b_ref, o_ref):
    c = cond_ref[...]
    o_ref[...] = _dot_hi(c * _sigmoid(c), w_ref[...]) + b_ref[...]


def _ada_call(cond, w_ada, b_ada):
    depth, d, n6 = w_ada.shape
    tn = 1024
    return pl.pallas_call(
        _ada_kernel,
        grid=(depth, n6 // tn),
        in_specs=[
            pl.BlockSpec((8, d), lambda l, j: (0, 0)),
            pl.BlockSpec((None, d, tn), lambda l, j: (l, 0, j)),
            pl.BlockSpec((None, 1, tn), lambda l, j: (l, 0, j)),
        ],
        out_specs=pl.BlockSpec((None, 8, tn), lambda l, j: (l, 0, j)),
        out_shape=jax.ShapeDtypeStruct((depth, 8, n6), F32),
        compiler_params=_cparams(2),
        name="ada",
    )(cond, w_ada, b_ada.reshape(depth, 1, n6))


def _mod_row(mod_ref, tpb, n_batch):
    i = pl.program_id(0)
    row = jnp.where(i % tpb == 0, n_batch, i // tpb)
    return mod_ref[pl.ds(row, 1), :]


IN_SEGS = (512, 512, 512, 128, 1024, 512, 512, 512)


def _inproj_kernel(x_ref, mod_ref, g_ref, w_ref, *outs, tpb, n_batch):
    d = x_ref.shape[1]
    mod = _mod_row(mod_ref, tpb, n_batch)
    h = _rms(x_ref[...], g_ref[...]) * (1.0 + mod[:, d:2 * d]) + mod[:, 0:d]
    hb = h.astype(BF16)
    off = 0
    for o_ref, n in zip(outs, IN_SEGS):
        o_ref[...] = _dot(hb, w_ref[:, off:off + n])
        off += n


def _inproj_call(x, mod_l, g, w_p, tpb, n_batch):
    n, d = x.shape
    nw = w_p.shape[1]
    return pl.pallas_call(
        functools.partial(_inproj_kernel, tpb=tpb, n_batch=n_batch),
        grid=(n // TILE,),
        in_specs=[
            pl.BlockSpec((TILE, d), lambda i: (i, 0)),
            pl.BlockSpec((8, 6 * d), lambda i: (0, 0)),
            pl.BlockSpec((1, d), lambda i: (0, 0)),
            pl.BlockSpec((d, nw), lambda i: (0, 0)),
        ],
        out_specs=[pl.BlockSpec((TILE, s), lambda i: (i, 0)) for s in IN_SEGS],
        out_shape=[jax.ShapeDtypeStruct((n, s), F32) for s in IN_SEGS],
        compiler_params=_cparams(1),
        name="inproj",
    )(x, mod_l, g.reshape(1, d), w_p)


def _conv_kernel(qk_m, qk_p, qk_n, qi_m, qi_p, qi_n, wq_ref, wi_ref, q_ref, k_ref, hq_ref, hv_ref, *, tpb):
    r = lax.broadcasted_iota(jnp.int32, (TILE, 1), 0)
    j = jnp.zeros((TILE, 1), jnp.int32) + pl.program_id(0) % tpb
    is_ctx = j == 0
    col = r % GRID_W
    ok_l = jnp.logical_or(col != 0, jnp.logical_and(is_ctx, r != 0))
    ok_r = jnp.logical_or(col != GRID_W - 1, jnp.logical_and(is_ctx, r != TILE - 1))
    ok_u = jnp.logical_and(jnp.logical_not(is_ctx), jnp.logical_or(j != 1, r >= GRID_W))
    ok_d = jnp.logical_and(jnp.logical_not(is_ctx), jnp.logical_or(j != tpb - 1, r < TILE - GRID_W))
    col_ok = (ok_l, None, ok_r)
    row_ok = (ok_u, None, ok_d)
    n_ext = TILE + 2 * GRID_W

    def conv(main, prev, nxt, w_ref, c0, c1):
        ext = jnp.concatenate([prev[:, c0:c1], main[:, c0:c1], nxt[:, c0:c1]], axis=0)
        shifted = (pltpu.roll(ext, 1, 0), ext, pltpu.roll(ext, n_ext - 1, 0))
        acc = jnp.zeros((TILE, c1 - c0), F32)
        for kh in range(3):
            for kw in range(3):
                tap = shifted[kw][kh * GRID_W:kh * GRID_W + TILE]
                ok = None
                for m in (row_ok[kh], col_ok[kw]):
                    if m is not None:
                        ok = m if ok is None else jnp.logical_and(ok, m)
                if ok is not None:
                    tap = jnp.where(ok, tap, 0.0)
                acc = acc + tap * w_ref[kh * 3 + kw:kh * 3 + kw + 1, c0:c1]
        return acc * _sigmoid(acc)

    nq = ML_HEADS * ML_QK
    q_ref[...] = conv(qk_m, qk_p, qk_n, wq_ref, 0, nq) * (ML_QK ** -0.5)
    k_ref[...] = conv(qk_m, qk_p, qk_n, wq_ref, nq, 2 * nq)
    nh = HG_HEADS * HG_DK
    hq_ref[...] = conv(qi_m, qi_p, qi_n, wi_ref, 0, nh)
    hv_ref[...] = conv(qi_m, qi_p, qi_n, wi_ref, nh, 2 * nh)


def _conv_call(qk, qi, wq, wi, tpb):
    n = qk.shape[0]
    rpt = TILE // GRID_W
    nrow = n // GRID_W
    cq, ci = qk.shape[1], qi.shape[1]

    def main(i):
        return (i, 0)

    def prev(i):
        return (jnp.maximum(i * rpt - 1, 0), 0)

    def nxt(i):
        return (jnp.minimum((i + 1) * rpt, nrow - 1), 0)

    return pl.pallas_call(
        functools.partial(_conv_kernel, tpb=tpb),
        grid=(n // TILE,),
        in_specs=[
            pl.BlockSpec((TILE, cq), main), pl.BlockSpec((GRID_W, cq), prev), pl.BlockSpec((GRID_W, cq), nxt),
            pl.BlockSpec((TILE, ci), main), pl.BlockSpec((GRID_W, ci), prev), pl.BlockSpec((GRID_W, ci), nxt),
            pl.BlockSpec((9, cq), lambda i: (0, 0)),
            pl.BlockSpec((9, ci), lambda i: (0, 0)),
        ],
        out_specs=[pl.BlockSpec((TILE, s), main) for s in (cq // 2, cq // 2, ci // 2, ci // 2)],
        out_shape=[jax.ShapeDtypeStruct((n, s), F32) for s in (cq // 2, cq // 2, ci // 2, ci // 2)],
        compiler_params=_cparams(1),
        name="conv",
    )(qk, qk, qk, qi, qi, qi, wq.reshape(9, cq), wi.reshape(9, ci))


W_CHUNKS = TILE // CHUNK


def _tile_maps(nct, nt):
    def fwd(b, j):
        return (b * nt + j, 0)

    def bwd(b, j):
        return (b * nt + jnp.where(j < nct, nct - 1 - j, nt + nct - 1 - j), 0)

    return fwd, bwd


def _tri(fwd):
    t = lax.broadcasted_iota(jnp.int32, (CHUNK, CHUNK), 0)
    s = lax.broadcasted_iota(jnp.int32, (CHUNK, CHUNK), 1)
    return (s <= t) if fwd else (s >= t)


def _mask_dot(mask_bf, x):
    a = x.astype(BF16)
    r = x - a.astype(F32)
    b = r.astype(BF16)
    c = (r - b.astype(F32)).astype(BF16)
    return (_dot(mask_bf, a) + _dot(mask_bf, b)) + _dot(mask_bf, c)


def _mlstm_kernel(qf, kf, vf, gf, qb, kb, vb, gb, bias_ref, hf_ref, hb_ref, ct_ref, m_ref):
    @pl.when(pl.program_id(1) == 0)
    def _():
        ct_ref[...] = jnp.zeros_like(ct_ref)
        m_ref[...] = jnp.zeros_like(m_ref)

    assert CHUNK == ML_QK and 2 * ML_QK == LANES
    lane = lax.broadcasted_iota(jnp.int32, (CHUNK, LANES), 1)
    t_id = lax.broadcasted_iota(jnp.int32, (CHUNK, LANES), 0)
    lo = lane < ML_QK
    lo_row = lo[0:1, :]
    s_id = lane % ML_QK
    r128 = lax.broadcasted_iota(jnp.int32, (LANES, LANES), 0)
    c128 = lax.broadcasted_iota(jnp.int32, (LANES, LANES), 1)
    same_head = (r128 // ML_QK) == (c128 // ML_QK)
    ones_blk = jnp.ones((CHUNK, LANES), BF16)
    zeros_va = jnp.zeros((CHUNK, ML_V + LANES), BF16)

    def rep(arr, col):
        return jnp.broadcast_to(arr[:, col:col + 1], (CHUNK, LANES))

    def half_max(x, first):
        sel = lo if x.shape[0] == CHUNK else lo_row
        y = jnp.where(sel if first else jnp.logical_not(sel), x, -jnp.inf)
        return jnp.broadcast_to(jnp.max(y, axis=-1, keepdims=True), x.shape)

    for d, (q_ref, k_ref, v_ref, g_ref, h_ref) in enumerate(((qf, kf, vf, gf, hf_ref), (qb, kb, vb, gb, hb_ref))):
        fwd = d == 0
        mask_bf = jnp.where(_tri(fwd), 1.0, 0.0).astype(BF16)
        mask_p = (s_id <= t_id) if fwd else (s_id >= t_id)
        last = CHUNK - 1 if fwd else 0
        order = range(W_CHUNKS) if fwd else range(W_CHUNKS - 1, -1, -1)
        local = {}
        for c in order:
            rows = slice(c * CHUNK, (c + 1) * CHUNK)
            g_all = g_ref[rows, :] + bias_ref[...]
            bcum = _mask_dot(mask_bf, _log_sigmoid(g_all))
            g_t = g_all.T
            b_t = bcum.T
            for p in range(ML_HEADS // 2):
                ci = [d * ML_HEADS + 2 * p + e for e in (0, 1)]
                cf = [(2 + d) * ML_HEADS + 2 * p + e for e in (0, 1)]
                b_rep = [rep(bcum, cf[e]) for e in (0, 1)]
                bp = jnp.where(lo, b_rep[0], b_rep[1])
                ip = jnp.where(lo, rep(g_all, ci[0]), rep(g_all, ci[1]))
                br = jnp.concatenate([b_t[cf[0]:cf[0] + 1, :], b_t[cf[1]:cf[1] + 1, :]], axis=1)
                ir = jnp.concatenate([g_t[ci[0]:ci[0] + 1, :], g_t[ci[1]:ci[1] + 1, :]], axis=1)
                dmat = jnp.where(mask_p, bp - br + ir, -jnp.inf)
                m_rep = [half_max(dmat, True), half_max(dmat, False)]
                q01 = q_ref[rows, p * LANES:(p + 1) * LANES].astype(BF16)
                k01 = k_ref[rows, p * LANES:(p + 1) * LANES]
                k01b = k01.astype(BF16)
                k_bd = jnp.where(same_head, jnp.concatenate([k01b, k01b], axis=0), jnp.zeros((), BF16))
                s = _dot_nt(q01, k_bd) * jnp.exp(dmat - jnp.where(lo, m_rep[0], m_rep[1]))
                va = [jnp.concatenate([v_ref[rows, (2 * p + e) * ML_V:(2 * p + e + 1) * ML_V].astype(BF16), ones_blk],
                                      axis=1) for e in (0, 1)]
                v_bd = jnp.concatenate([jnp.concatenate([va[0], zeros_va], axis=1),
                                        jnp.concatenate([zeros_va, va[1]], axis=1)], axis=0)
                nd_loc = _dot(s.astype(BF16), v_bd)
                gp = bp[last:last + 1, :]
                e_row = gp - br + ir
                me_rep = [half_max(e_row, True), half_max(e_row, False)]
                kw = (k01 * jnp.exp(gp - bp + ip - jnp.where(lo_row, me_rep[0], me_rep[1]))).astype(BF16)
                upd = _dot_tn(kw, jnp.concatenate(va, axis=1))
                g_tot = [b_rep[e][last:last + 1, :] for e in (0, 1)]
                local[(c, p)] = (q01, nd_loc, upd, b_rep, m_rep, g_tot, me_rep)
        wv = ML_V + LANES
        for p in range(ML_HEADS // 2):
            ci = [d * ML_HEADS + 2 * p + e for e in (0, 1)]
            m = [m_ref[ci[e]:ci[e] + 1, :] for e in (0, 1)]
            ct = [ct_ref[ci[e]] for e in (0, 1)]
            for c in order:
                q01, nd_loc, upd, b_rep, m_rep, g_tot, me_rep = local[(c, p)]
                ct_both = jnp.concatenate(ct, axis=0).astype(BF16)
                for e in (0, 1):
                    h = 2 * p + e
                    q_e = jnp.where(lo if e == 0 else jnp.logical_not(lo), q01, jnp.zeros((), BF16))
                    qc = _dot(q_e, ct_both)
                    a_inter = b_rep[e] + m[e]
                    m_t = jnp.maximum(a_inter, m_rep[e])
                    w_loc = jnp.exp(m_rep[e] - m_t)
                    w_int = jnp.exp(a_inter - m_t)
                    num = w_loc * nd_loc[:, e * wv:e * wv + ML_V] + w_int * qc[:, 0:ML_V]
                    den = w_loc * nd_loc[:, e * wv + ML_V:(e + 1) * wv] + w_int * qc[:, ML_V:wv]
                    h_ref[c * CHUNK:(c + 1) * CHUNK, h * ML_V:(h + 1) * ML_V] = (
                        num / jnp.maximum(jnp.abs(den), jnp.exp(-m_t)))
                    m_new = jnp.maximum(g_tot[e] + m[e], me_rep[e])
                    w_c = jnp.exp(g_tot[e] + m[e] - m_new)
                    w_u = jnp.exp(me_rep[e] - m_new)
                    ct[e] = (jnp.concatenate([w_c, w_c], axis=1) * ct[e]
                             + jnp.concatenate([w_u, w_u], axis=1) * upd[e * ML_QK:(e + 1) * ML_QK, e * wv:(e + 1) * wv])
                    m[e] = m_new
            for e in (0, 1):
                ct_ref[ci[e]] = ct[e]
                m_ref[ci[e]:ci[e] + 1, :] = m[e]


def _mlstm_call(q, k, v, gates, gate_b, n_batch, nct):
    n = q.shape[0]
    nt = n // TILE // n_batch
    fwd, bwd = _tile_maps(nct, nt)
    dq, dv = q.shape[1], v.shape[1]
    specs = lambda m: [pl.BlockSpec((TILE, dq), m), pl.BlockSpec((TILE, dq), m),
                       pl.BlockSpec((TILE, dv), m), pl.BlockSpec((TILE, LANES), m)]
    bias = jnp.zeros((1, LANES), F32).at[0, :4 * ML_HEADS].set(gate_b.reshape(-1))
    return pl.pallas_call(
        _mlstm_kernel,
        grid=(n_batch, nt),
        in_specs=specs(fwd) + specs(bwd) + [pl.BlockSpec((1, LANES), lambda b, j: (0, 0))],
        out_specs=[pl.BlockSpec((TILE, dv), fwd), pl.BlockSpec((TILE, dv), bwd)],
        out_shape=[jax.ShapeDtypeStruct((n, dv), F32)] * 2,
        scratch_shapes=[pltpu.VMEM((2 * ML_HEADS, ML_QK, ML_V + LANES), F32), pltpu.VMEM((2 * ML_HEADS, LANES), F32)],
        compiler_params=_cparams(2),
        name="mlstm",
    )(q, k, v, gates, q, k, v, gates, bias)


def _hgrn_kernel(qf, vf, ff, qb, vb, fb, fbias_ref, lb_ref, of_ref, ob_ref, st_ref):
    @pl.when(pl.program_id(1) == 0)
    def _():
        st_ref[...] = jnp.zeros_like(st_ref)

    n_sub = CHUNK // SUB
    rows = lax.broadcasted_iota(jnp.int32, (CHUNK, 1), 0)
    t_i = lax.broadcasted_iota(jnp.int32, (CHUNK, CHUNK), 0)
    s_i = lax.broadcasted_iota(jnp.int32, (CHUNK, CHUNK), 1)
    same_sub = (t_i // SUB) == (s_i // SUB)
    lb = lb_ref[...]
    for d, (q_ref, v_ref, f_ref, o_ref) in enumerate(((qf, vf, ff, of_ref), (qb, vb, fb, ob_ref))):
        fwd = d == 0
        mask = _tri(fwd)
        mask_bf = jnp.where(mask, 1.0, 0.0).astype(BF16)
        diag_ok = jnp.logical_and(same_sub, mask)
        last = CHUNK - 1 if fwd else 0
        order = range(W_CHUNKS) if fwd else range(W_CHUNKS - 1, -1, -1)
        local = {}
        for c in order:
            rs = slice(c * CHUNK, (c + 1) * CHUNK)
            f = lb + (1.0 - lb) * _sigmoid(f_ref[rs, :] + fbias_ref[d:d + 1, :])
            kk = 1.0 - f
            b = _mask_dot(mask_bf, jnp.log(f))
            q = q_ref[rs, :]
            q_parts, k_parts = [], []
            for p in range(1, n_sub):
                if fwd:
                    ref_row, q_ok, k_ok = p * SUB - 1, (rows // SUB) == p, rows < p * SUB
                else:
                    ref_row, q_ok, k_ok = p * SUB, (rows // SUB) == p - 1, rows >= p * SUB
                r_p = b[ref_row:ref_row + 1, :]
                q_parts.append(jnp.where(q_ok, q * jnp.exp(jnp.minimum(b - r_p, 0.0)), 0.0))
                k_parts.append(jnp.where(k_ok, kk * jnp.exp(jnp.minimum(r_p - b, 0.0)), 0.0))
            mid = SUB // 2 - 1 if fwd else SUB // 2
            c_m = b[mid:mid + 1, :]
            for i in range(1, n_sub):
                c_m = jnp.where(rows >= i * SUB, b[i * SUB + mid:i * SUB + mid + 1, :], c_m)
            q_d = q * jnp.exp(b - c_m)
            k_d = kk * jnp.exp(c_m - b)
            g_row = b[last:last + 1, :]
            q_in = (q * jnp.exp(b)).astype(BF16)
            k_out = (kk * jnp.exp(g_row - b)).astype(BF16)
            decay = jnp.exp(g_row)
            for h in range(HG_HEADS):
                sl = slice(h * HG_DK, (h + 1) * HG_DK)
                qc = jnp.concatenate([x[:, sl] for x in q_parts], axis=1).astype(BF16)
                kc = jnp.concatenate([x[:, sl] for x in k_parts], axis=1).astype(BF16)
                a = _dot_nt(qc, kc) + jnp.where(diag_ok, _dot_nt(q_d[:, sl].astype(BF16), k_d[:, sl].astype(BF16)), 0.0)
                vh = v_ref[rs, sl].astype(BF16)
                local[(c, h)] = (_dot(a.astype(BF16), vh), q_in[:, sl], decay[:, sl], _dot_tn(vh, k_out[:, sl]))
        for h in range(HG_HEADS):
            st = st_ref[d * HG_HEADS + h]
            for c in order:
                o_loc, q_in, decay, upd = local[(c, h)]
                o_ref[c * CHUNK:(c + 1) * CHUNK, h * HG_DK:(h + 1) * HG_DK] = o_loc + _dot_nt(q_in, st.astype(BF16))
                st = st * decay + upd
            st_ref[d * HG_HEADS + h] = st


def _hgrn_call(hq, hv, ff, fb, f_b, lb, n_batch, nct):
    n, w = hq.shape
    nt = n // TILE // n_batch
    fwd, bwd = _tile_maps(nct, nt)
    spec = lambda m: pl.BlockSpec((TILE, w), m)
    return pl.pallas_call(
        _hgrn_kernel,
        grid=(n_batch, nt),
        in_specs=[spec(fwd)] * 3 + [spec(bwd)] * 3 + [pl.BlockSpec((2, w), lambda b, j: (0, 0)),
                                                      pl.BlockSpec((1, w), lambda b, j: (0, 0))],
        out_specs=[spec(fwd), spec(bwd)],
        out_shape=[jax.ShapeDtypeStruct((n, w), F32)] * 2,
        scratch_shapes=[pltpu.VMEM((2 * HG_HEADS, HG_DK, HG_DK), F32)],
        compiler_params=_cparams(2),
        name="hgrn",
    )(hq, hv, ff, hq, hv, fb, f_b, lb.reshape(1, w))


def _head_rms(h, g, width):
    parts = []
    for i in range(h.shape[1] // width):
        p = h[:, i * width:(i + 1) * width]
        parts.append(p * lax.rsqrt(jnp.mean(p * p, axis=-1, keepdims=True) + EPS))
    return jnp.concatenate(parts, axis=1) * g


def _out_kernel(x_ref, mlf, mlb, hgf, hgb, o_ref, go_ref, mod_ref, gml_ref, ghg_ref, wout_ref, g2_ref, rw_ref, rb_ref,
                xo_ref, h2_ref, idx_ref, gate_ref, *, tpb, n_batch):
    d = x_ref.shape[1]
    mod = _mod_row(mod_ref, tpb, n_batch)
    y_ml = _sigmoid(o_ref[...]) * _head_rms(mlf[...] + mlb[...], gml_ref[...], ML_V)
    go = go_ref[...]
    y_hg = go * _sigmoid(go) * _head_rms(hgf[...] + hgb[...], ghg_ref[...], HG_DK)
    y = jnp.concatenate([y_ml, y_hg], axis=1).astype(BF16)
    x = x_ref[...] + mod[:, 2 * d:3 * d] * _dot(y, wout_ref[...])
    xo_ref[...] = x
    h2 = _rms(x, g2_ref[...]) * (1.0 + mod[:, 4 * d:5 * d]) + mod[:, 3 * d:4 * d]
    h2_ref[...] = h2
    vals = _dot_hi(h2, rw_ref[...]) + rb_ref[...]
    lane = lax.broadcasted_iota(jnp.int32, vals.shape, 1)
    lane_f = lane.astype(F32)
    idx_out = jnp.zeros(vals.shape, F32)
    top = []
    for k in range(TOP_K):
        mx = jnp.max(vals, axis=-1, keepdims=True)
        ix = jnp.min(jnp.where(vals == mx, lane_f, float(LANES)), axis=-1, keepdims=True)
        top.append(mx)
        idx_out = jnp.where(lane == k, ix, idx_out)
        vals = jnp.where(lane_f == ix, -jnp.inf, vals)
    ex = [jnp.exp(t - top[0]) for t in top]
    tot = ex[0] + ex[1] + ex[2] + ex[3]
    gate_out = jnp.zeros(vals.shape, F32)
    for k in range(TOP_K):
        gate_out = jnp.where(lane == k, ex[k] / tot, gate_out)
    idx_ref[...] = idx_out.astype(jnp.int32)
    gate_ref[...] = gate_out


def _out_call(x, mlf, mlb, hgf, hgb, o, go, mod_l, gml, ghg, wout, g2, rw, rb, tpb, n_batch):
    n, d = x.shape
    w = mlf.shape[1]
    row = lambda i: (i, 0)
    fix = lambda i: (0, 0)
    rw_p = jnp.zeros((d, LANES), F32).at[:, :N_EXPERTS].set(rw)
    rb_p = jnp.full((1, LANES), NEG, F32).at[0, :N_EXPERTS].set(rb)
    return pl.pallas_call(
        functools.partial(_out_kernel, tpb=tpb, n_batch=n_batch),
        grid=(n // TILE,),
        in_specs=[pl.BlockSpec((TILE, d), row)] + [pl.BlockSpec((TILE, w), row)] * 6 + [
            pl.BlockSpec((8, 6 * d), fix), pl.BlockSpec((1, w), fix), pl.BlockSpec((1, w), fix),
            pl.BlockSpec((d, d), fix), pl.BlockSpec((1, d), fix), pl.BlockSpec((d, LANES), fix),
            pl.BlockSpec((1, LANES), fix)],
        out_specs=[pl.BlockSpec((TILE, d), row), pl.BlockSpec((TILE, d), row),
                   pl.BlockSpec((TILE, LANES), row), pl.BlockSpec((TILE, LANES), row)],
        out_shape=[jax.ShapeDtypeStruct((n, d), F32), jax.ShapeDtypeStruct((n, d), F32),
                   jax.ShapeDtypeStruct((n, LANES), jnp.int32), jax.ShapeDtypeStruct((n, LANES), F32)],
        compiler_params=_cparams(1),
        name="outproj",
    )(x, mlf, mlb, hgf, hgb, o, go, mod_l, gml.reshape(1, w), ghg.reshape(1, w), wout, g2.reshape(1, d), rw_p, rb_p)


def _onehots(idx):
    lane = lax.broadcasted_iota(jnp.int32, idx.shape, 1)
    return [lane == idx[:, k:k + 1] for k in range(TOP_K)]


def _rank_kernel(idx_ref, rank_ref, cnt_ref, carry_ref):
    @pl.when(pl.program_id(0) == 0)
    def _():
        carry_ref[...] = jnp.zeros_like(carry_ref)

    hots = _onehots(idx_ref[...])
    m = jnp.zeros(idx_ref.shape, F32)
    for hk in hots:
        m = m + jnp.where(hk, 1.0, 0.0)
    t_i = lax.broadcasted_iota(jnp.int32, (TILE, TILE), 0)
    s_i = lax.broadcasted_iota(jnp.int32, (TILE, TILE), 1)
    before = _dot((s_i < t_i).astype(BF16), m.astype(BF16)) + carry_ref[...]
    lane = lax.broadcasted_iota(jnp.int32, idx_ref.shape, 1)
    out = jnp.zeros(idx_ref.shape, F32)
    for k, hk in enumerate(hots):
        rk = jnp.sum(jnp.where(hk, before, 0.0), axis=-1, keepdims=True)
        out = jnp.where(lane == k, rk, out)
    rank_ref[...] = out.astype(jnp.int32)
    carry_ref[...] = carry_ref[...] + jnp.sum(m, axis=0, keepdims=True)
    cnt_ref[...] = carry_ref[...]


def _rank_call(idx):
    n = idx.shape[0]
    return pl.pallas_call(
        _rank_kernel,
        grid=(n // TILE,),
        in_specs=[pl.BlockSpec((TILE, LANES), lambda i: (i, 0))],
        out_specs=[pl.BlockSpec((TILE, LANES), lambda i: (i, 0)), pl.BlockSpec((1, LANES), lambda i: (0, 0))],
        out_shape=[jax.ShapeDtypeStruct((n, LANES), jnp.int32), jax.ShapeDtypeStruct((1, LANES), F32)],
        scratch_shapes=[pltpu.VMEM((1, LANES), F32)],
        compiler_params=_cparams(1),
        name="rank",
    )(idx)


def _dest_kernel(idx_ref, rank_ref, cnt_ref, dest_ref, meta_ref, *, n_blocks):
    cnt = cnt_ref[...]
    padded = jnp.floor((cnt + (MOE_BLK - 1)) * (1.0 / MOE_BLK)) * MOE_BLK
    e_i = lax.broadcasted_iota(jnp.int32, (LANES, LANES), 0)
    e_j = lax.broadcasted_iota(jnp.int32, (LANES, LANES), 1)
    pad_start = _dot_hi(jnp.broadcast_to(padded, (8, LANES)), (e_i < e_j).astype(F32))[0:1]
    pad_end = pad_start + padded
    lane = lax.broadcasted_iota(jnp.int32, idx_ref.shape, 1)
    rank = rank_ref[...]
    out = jnp.zeros(idx_ref.shape, jnp.int32)
    for k, hk in enumerate(_onehots(idx_ref[...])):
        st = jnp.sum(jnp.where(hk, pad_start, 0.0), axis=-1, keepdims=True).astype(jnp.int32)
        out = jnp.where(lane == k, st + rank[:, k:k + 1], out)
    dest_ref[...] = out
    blk = lax.broadcasted_iota(jnp.int32, meta_ref.shape, 0).astype(F32) * MOE_BLK
    lane_m = lax.broadcasted_iota(jnp.int32, meta_ref.shape, 1)
    done = jnp.where(jnp.logical_and(lane_m < N_EXPERTS, pad_end <= blk), 1.0, 0.0)
    be = jnp.minimum(jnp.sum(done, axis=-1, keepdims=True), N_EXPERTS - 1.0)
    used = jnp.sum(jnp.where(lane_m[0:1] < N_EXPERTS, padded, 0.0), axis=-1, keepdims=True) * (1.0 / MOE_BLK)
    diag = lane_m == lax.broadcasted_iota(jnp.int32, meta_ref.shape, 0)
    end_col = jnp.sum(jnp.where(diag, pad_end, 0.0), axis=-1, keepdims=True)
    pad_col = jnp.sum(jnp.where(diag, padded, 0.0), axis=-1, keepdims=True)
    meta = jnp.where(lane_m == 0, be, jnp.where(lane_m == 1, used, jnp.where(lane_m == 2, end_col, pad_col)))
    meta_ref[...] = jnp.where(lane_m < 4, meta, 0.0).astype(jnp.int32)


def _dest_call(idx, rank, cnt, n_blocks):
    n = idx.shape[0]
    nb_pad = -(-n_blocks // 8) * 8
    row = lambda i: (i, 0)
    return pl.pallas_call(
        functools.partial(_dest_kernel, n_blocks=n_blocks),
        grid=(n // TILE,),
        in_specs=[pl.BlockSpec((TILE, LANES), row), pl.BlockSpec((TILE, LANES), row),
                  pl.BlockSpec((1, LANES), lambda i: (0, 0))],
        out_specs=[pl.BlockSpec((TILE, LANES), row), pl.BlockSpec((nb_pad, LANES), lambda i: (0, 0))],
        out_shape=[jax.ShapeDtypeStruct((n, LANES), jnp.int32), jax.ShapeDtypeStruct((nb_pad, LANES), jnp.int32)],
        compiler_params=_cparams(1),
        name="dest",
    )(idx, rank, cnt)


ROW_UNROLL = 8


def _scatter_kernel(dest_ref, zinfo_ref, h_ref, xb_ref, zbuf, sem):
    @pl.when(pl.program_id(0) == 0)
    def _():
        zbuf[...] = jnp.zeros_like(zbuf)
        n_blocks = xb_ref.shape[0] // MOE_BLK
        used = zinfo_ref[2 * N_EXPERTS]
        for stage in ("start", "wait"):
            for e in range(N_EXPERTS):
                @pl.when(zinfo_ref[2 * e + 1] > 0)
                def _():
                    first = pl.multiple_of(zinfo_ref[2 * e] - MOE_BLK, MOE_BLK)
                    cp = pltpu.make_async_copy(zbuf, xb_ref.at[pl.ds(first, MOE_BLK), :], sem)
                    cp.start() if stage == "start" else cp.wait()

                @pl.when(used + e < n_blocks)
                def _():
                    first = pl.multiple_of((used + e) * MOE_BLK, MOE_BLK)
                    cp = pltpu.make_async_copy(zbuf, xb_ref.at[pl.ds(first, MOE_BLK), :], sem)
                    cp.start() if stage == "start" else cp.wait()

    def issue(g, c):
        for u in range(ROW_UNROLL):
            r = g * ROW_UNROLL + u
            for k in range(TOP_K):
                dst = dest_ref[g * (ROW_UNROLL * TOP_K) + u * TOP_K + k]
                pltpu.make_async_copy(h_ref.at[pl.ds(r, 1), :], xb_ref.at[pl.ds(dst, 1), :], sem).start(priority=k % 2)
        return c

    lax.fori_loop(0, TILE // ROW_UNROLL, issue, 0)
    all_rows = xb_ref.at[pl.ds(0, TILE * TOP_K), :]
    pltpu.make_async_copy(all_rows, all_rows, sem).wait()


def _scatter_call(dest_flat, zinfo, h2, n_rows):
    n, d = h2.shape
    return pl.pallas_call(
        _scatter_kernel,
        grid=(n // TILE,),
        in_specs=[pl.BlockSpec((TILE * TOP_K,), lambda i: (i,), memory_space=pltpu.SMEM),
                  pl.BlockSpec(memory_space=pltpu.SMEM),
                  pl.BlockSpec((TILE, d), lambda i: (i, 0))],
        out_specs=pl.BlockSpec(memory_space=pl.ANY),
        out_shape=jax.ShapeDtypeStruct((n_rows, d), F32),
        scratch_shapes=[pltpu.VMEM((MOE_BLK, d), F32), pltpu.SemaphoreType.DMA(())],
        compiler_params=_cparams(1),
        name="scatter",
    )(dest_flat, zinfo, h2)


CAST_ROWS = 64


def _cast_rows(src_ref, dst_ref):
    def body(r, c):
        rows = pl.ds(pl.multiple_of(r * CAST_ROWS, CAST_ROWS), CAST_ROWS)
        dst_ref[rows, :] = src_ref[rows, :].astype(BF16)
        return c

    lax.fori_loop(0, src_ref.shape[0] // CAST_ROWS, body, 0)


def _expert_kernel(be_ref, used_ref, x_ref, wgu_ref, bgu_ref, wd_ref, bd_ref, y_ref, wgu_bf, wd_bf):
    i = pl.program_id(0)
    live = i < used_ref[0]
    new_expert = jnp.logical_or(i == 0, be_ref[i] != be_ref[jnp.maximum(i - 1, 0)])

    @pl.when(jnp.logical_not(live))
    def _():
        y_ref[...] = jnp.zeros_like(y_ref)

    @pl.when(jnp.logical_and(live, new_expert))
    def _():
        _cast_rows(wgu_ref, wgu_bf)
        _cast_rows(wd_ref, wd_bf)

    @pl.when(live)
    def _():
        de = wd_ref.shape[0]
        gu = _dot(x_ref[...].astype(BF16), wgu_bf[...]) + bgu_ref[...]
        glu = jnp.minimum(gu[:, :de], SWIGLU_LIMIT)
        lin = jnp.clip(gu[:, de:], -SWIGLU_LIMIT, SWIGLU_LIMIT)
        act = glu * _sigmoid(SWIGLU_ALPHA * glu) * (lin + 1.0)
        y_ref[...] = _dot(act.astype(BF16), wd_bf[...]) + bd_ref[...]


def _expert_call(layer, be, used, xb, wgu, bgu, wd, bd):
    n_rows, d = xb.shape
    depth, e, _, de2 = wgu.shape
    nb = n_rows // MOE_BLK
    last_live = lambda i, used: jnp.maximum(jnp.minimum(i, used[0] - 1), 0)
    blk = lambda i, be, used: (last_live(i, used), 0)
    wsel = lambda i, be, used: (layer, be[last_live(i, used)], 0, 0)
    return pl.pallas_call(
        _expert_kernel,
        grid_spec=pltpu.PrefetchScalarGridSpec(
            num_scalar_prefetch=2,
            grid=(nb,),
            in_specs=[pl.BlockSpec((MOE_BLK, d), blk),
                      pl.BlockSpec((None, None, d, de2), wsel), pl.BlockSpec((None, None, 1, de2), wsel),
                      pl.BlockSpec((None, None, de2 // 2, d), wsel), pl.BlockSpec((None, None, 1, d), wsel)],
            out_specs=pl.BlockSpec((MOE_BLK, d), lambda i, be, used: (i, 0)),
            scratch_shapes=[pltpu.VMEM((d, de2), BF16), pltpu.VMEM((de2 // 2, d), BF16)],
        ),
        out_shape=jax.ShapeDtypeStruct((n_rows, d), F32),
        compiler_params=_cparams(1, vmem=EXPERT_VMEM_LIMIT),
        name="expert",
    )(be, used, xb, wgu, bgu.reshape(depth, e, 1, de2), wd, bd.reshape(depth, e, 1, d))


def _combine_kernel(dest_ref, x_ref, gate_ref, mod_ref, fg_ref, yb_ref, o_ref, buf, sem, *, tpb, n_batch, final):
    def issue(g, c):
        for u in range(ROW_UNROLL):
            r = g * ROW_UNROLL + u
            for k in range(TOP_K):
                src = dest_ref[g * (ROW_UNROLL * TOP_K) + u * TOP_K + k]
                pltpu.make_async_copy(yb_ref.at[pl.ds(src, 1), :], buf.at[k, pl.ds(r, 1), :], sem).start(priority=k % 2)
        return c

    lax.fori_loop(0, TILE // ROW_UNROLL, issue, 0)
    pltpu.make_async_copy(buf, buf, sem).wait()
    d = x_ref.shape[1]
    mod = _mod_row(mod_ref, tpb, n_batch)
    gates = gate_ref[...]
    f = gates[:, 0:1] * buf[0]
    for k in range(1, TOP_K):
        f = f + gates[:, k:k + 1] * buf[k]
    x = x_ref[...] + mod[:, 5 * d:6 * d] * f
    o_ref[...] = _rms(x, fg_ref[...]) if final else x


def _combine_call(dest_flat, x, gates, mod_l, fg, yb, tpb, n_batch, final):
    n, d = x.shape
    if final:
        out_rows = n - n_batch * TILE
        omap = lambda i: ((i // tpb) * (tpb - 1) + jnp.maximum(i % tpb - 1, 0), 0)
    else:
        out_rows = n
        omap = lambda i: (i, 0)
    return pl.pallas_call(
        functools.partial(_combine_kernel, tpb=tpb, n_batch=n_batch, final=final),
        grid=(n // TILE,),
        in_specs=[pl.BlockSpec((TILE * TOP_K,), lambda i: (i,), memory_space=pltpu.SMEM),
                  pl.BlockSpec((TILE, d), lambda i: (i, 0)),
                  pl.BlockSpec((TILE, LANES), lambda i: (i, 0)),
                  pl.BlockSpec((8, 6 * d), lambda i: (0, 0)),
                  pl.BlockSpec((1, d), lambda i: (0, 0)),
                  pl.BlockSpec(memory_space=pl.ANY)],
        out_specs=pl.BlockSpec((TILE, d), omap),
        out_shape=jax.ShapeDtypeStruct((out_rows, d), F32),
        scratch_shapes=[pltpu.VMEM((TOP_K, TILE, d), F32), pltpu.SemaphoreType.DMA(())],
        compiler_params=_cparams(1),
        name="combine",
    )(dest_flat, x, gates, mod_l, fg.reshape(1, d), yb)


def kernel(x, c, ctx, c_ctx, w_ada, b_ada, norm1_g, w_in, mlstm_conv, mlstm_gate_b, mlstm_norm_g, hgrn_conv, hgrn_f_b,
           hgrn_lb_raw, hgrn_norm_g, w_out, norm2_g, router_w, router_b, w_gu, b_gu, w_down, b_down, final_g):
    n_batch, seq, d = x.shape
    ctx_len = ctx.shape[1]
    depth = w_ada.shape[0]
    assert ctx_len == TILE and seq % TILE == 0 and n_batch + 1 <= 8
    tpb = (ctx_len + seq) // TILE
    nct = ctx_len // TILE
    n = n_batch * (ctx_len + seq)
    n_blocks = -(-(n * TOP_K) // MOE_BLK) + N_EXPERTS
    n_rows = n_blocks * MOE_BLK

    xa = jnp.concatenate([ctx, x], axis=1).reshape(n, d)
    cond = jnp.zeros((8, d), F32).at[:n_batch].set(c).at[n_batch].set(c_ctx)
    mod = _ada_call(cond, w_ada, b_ada)

    lb_w = jax.nn.softmax(hgrn_lb_raw.astype(F32), axis=0)
    lower = jnp.cumsum(lb_w, axis=0) - lb_w[0]

    n_gate = 4 * ML_HEADS
    g0 = 3 * 512
    w_in_p = jnp.concatenate([w_in[:, :, :g0], jnp.pad(w_in[:, :, g0:g0 + n_gate], ((0, 0), (0, 0), (0, LANES - n_gate))),
                              w_in[:, :, g0 + n_gate:]], axis=2).astype(BF16)
    w_out_b = w_out.astype(BF16)

    for l in range(depth):
        last = l == depth - 1
        qk, v, o, gates, qi, ff, fb, go = _inproj_call(xa, mod[l], norm1_g[l], w_in_p[l], tpb, n_batch)
        q, k, hq, hv = _conv_call(qk, qi, mlstm_conv[l], hgrn_conv[l], tpb)
        mlf, mlb = _mlstm_call(q, k, v, gates, mlstm_gate_b[l], n_batch, nct)
        hgf, hgb = _hgrn_call(hq, hv, ff, fb, hgrn_f_b[l], lower[l], n_batch, nct)
        xa, h2, idx, gate = _out_call(xa, mlf, mlb, hgf, hgb, o, go, mod[l], mlstm_norm_g[l], hgrn_norm_g[l], w_out_b[l],
                                      norm2_g[l], router_w[l], router_b[l], tpb, n_batch)
        rank, cnt = _rank_call(idx)
        dest, meta = _dest_call(idx, rank, cnt, n_blocks)
        dest_flat = dest[:, :TOP_K].reshape(-1)
        zinfo = jnp.concatenate([meta[:N_EXPERTS, 2:4].reshape(-1), meta[0:1, 1]])
        xb = _scatter_call(dest_flat, zinfo, h2, n_rows)
        yb = _expert_call(l, meta[:n_blocks, 0], meta[0:1, 1], xb, w_gu, b_gu, w_down, b_down)
        xa = _combine_call(dest_flat, xa, gate, mod[l], final_g, yb, tpb, n_batch, last)
    return xa.reshape(n_batch, seq, d)
```

```python
import functools

import jax
import jax.numpy as jnp
from jax import lax
from jax.experimental import pallas as pl
from jax.experimental.pallas import tpu as pltpu

F32 = jnp.float32
BF16 = jnp.bfloat16
HIGHEST = lax.Precision.HIGHEST

GRID_W = 64
CHUNK = 64
ML_HEADS = 4
ML_QK = 64
ML_V = 128
HG_HEADS = 4
HG_DK = 128
N_EXPERTS = 32
TOP_K = 4
SWIGLU_LIMIT = 7.0
SWIGLU_ALPHA = 1.702
EPS = 1e-6

TILE = 256
OUT_TILES = 2
MOE_BLK = 256
LANES = 128
SUB = 16
NEG = -1e30
VMEM_LIMIT = 48 * 1024 * 1024
EXPERT_VMEM_LIMIT = 56 * 1024 * 1024


def _cparams(n_axes, vmem=VMEM_LIMIT):
    return pltpu.CompilerParams(dimension_semantics=("arbitrary",) * n_axes, vmem_limit_bytes=vmem)


def _dot(a, b):
    return jnp.dot(a, b, preferred_element_type=F32)


def _dot_nt(a, b):
    return lax.dot_general(a, b, (((1,), (1,)), ((), ())), preferred_element_type=F32)


def _dot_tn(a, b):
    return lax.dot_general(a, b, (((0,), (0,)), ((), ())), preferred_element_type=F32)


def _dot_hi(a, b):
    return jnp.dot(a, b, precision=HIGHEST, preferred_element_type=F32)


def _rms(x, g):
    return x * lax.rsqrt(jnp.mean(x * x, axis=-1, keepdims=True) + EPS) * g


def _sigmoid(x):
    return 1.0 / (1.0 + jnp.exp(-x))


def _log_sigmoid(x):
    return jnp.minimum(x, 0.0) - jnp.log(1.0 + jnp.exp(-jnp.abs(x)))


SLAB = 8


def _store_slabs(ref, x, base=0):
    rows = x.shape[0]
    for s in range(SLAB):
        ref[pl.ds(base + s, rows, stride=SLAB), :] = x[:, s * LANES:(s + 1) * LANES]


def _load_slabs(ref, rows):
    return jnp.concatenate([ref[pl.ds(s, rows, stride=SLAB), :] for s in range(SLAB)], axis=1)


def _ada_kernel(cond_ref, w_ref, b_ref, o_ref):
    c = cond_ref[...]
    o_ref[...] = _dot_hi(c * _sigmoid(c), w_ref[...]) + b_ref[...]


def _ada_call(cond, w_ada, b_ada):
    depth, d, n6 = w_ada.shape
    tn = 1024
    return pl.pallas_call(
        _ada_kernel,
        grid=(depth, n6 // tn),
        in_specs=[
            pl.BlockSpec((8, d), lambda l, j: (0, 0)),
            pl.BlockSpec((None, d, tn), lambda l, j: (l, 0, j)),
            pl.BlockSpec((None, 1, tn), lambda l, j: (l, 0, j)),
        ],
        out_specs=pl.BlockSpec((None, 8, tn), lambda l, j: (l, 0, j)),
        out_shape=jax.ShapeDtypeStruct((depth, 8, n6), F32),
        compiler_params=_cparams(2),
        name="ada",
    )(cond, w_ada, b_ada.reshape(depth, 1, n6))


def _mod_row(mod_ref, tpb, n_batch, tile=None):
    i = pl.program_id(0) if tile is None else tile
    row = jnp.where(i % tpb == 0, n_batch, i // tpb)
    return mod_ref[pl.ds(row, 1), :]


IN_SEGS = (512, 512, 512, 128, 1024, 512, 512, 512)


def _inproj_kernel(x_ref, mod_ref, g_ref, w_ref, *outs, tpb, n_batch):
    d = x_ref.shape[1]
    mod = _mod_row(mod_ref, tpb, n_batch)
    h = _rms(x_ref[...], g_ref[...]) * (1.0 + mod[:, d:2 * d]) + mod[:, 0:d]
    hb = h.astype(BF16)
    off = 0
    for o_ref, n in zip(outs, IN_SEGS):
        o_ref[...] = _dot(hb, w_ref[:, off:off + n])
        off += n


def _inproj_call(x, mod_l, g, w_p, tpb, n_batch):
    n, d = x.shape
    nw = w_p.shape[1]
    return pl.pallas_call(
        functools.partial(_inproj_kernel, tpb=tpb, n_batch=n_batch),
        grid=(n // TILE,),
        in_specs=[
            pl.BlockSpec((TILE, d), lambda i: (i, 0)),
            pl.BlockSpec((8, 6 * d), lambda i: (0, 0)),
            pl.BlockSpec((1, d), lambda i: (0, 0)),
            pl.BlockSpec((d, nw), lambda i: (0, 0)),
        ],
        out_specs=[pl.BlockSpec((TILE, s), lambda i: (i, 0)) for s in IN_SEGS],
        out_shape=[jax.ShapeDtypeStruct((n, s), F32) for s in IN_SEGS],
        compiler_params=_cparams(1),
        name="inproj",
    )(x, mod_l, g.reshape(1, d), w_p)


def _conv_kernel(qk_m, qk_p, qk_n, qi_m, qi_p, qi_n, wq_ref, wi_ref, q_ref, k_ref, hq_ref, hv_ref, *, tpb):
    r = lax.broadcasted_iota(jnp.int32, (TILE, 1), 0)
    j = jnp.zeros((TILE, 1), jnp.int32) + pl.program_id(0) % tpb
    is_ctx = j == 0
    col = r % GRID_W
    ok_l = jnp.logical_or(col != 0, jnp.logical_and(is_ctx, r != 0))
    ok_r = jnp.logical_or(col != GRID_W - 1, jnp.logical_and(is_ctx, r != TILE - 1))
    ok_u = jnp.logical_and(jnp.logical_not(is_ctx), jnp.logical_or(j != 1, r >= GRID_W))
    ok_d = jnp.logical_and(jnp.logical_not(is_ctx), jnp.logical_or(j != tpb - 1, r < TILE - GRID_W))
    col_ok = (ok_l, None, ok_r)
    row_ok = (ok_u, None, ok_d)
    n_ext = TILE + 2 * GRID_W

    def conv(main, prev, nxt, w_ref, c0, c1):
        ext = jnp.concatenate([prev[:, c0:c1], main[:, c0:c1], nxt[:, c0:c1]], axis=0)
        shifted = (pltpu.roll(ext, 1, 0), ext, pltpu.roll(ext, n_ext - 1, 0))
        acc = jnp.zeros((TILE, c1 - c0), F32)
        for kh in range(3):
            for kw in range(3):
                tap = shifted[kw][kh * GRID_W:kh * GRID_W + TILE]
                ok = None
                for m in (row_ok[kh], col_ok[kw]):
                    if m is not None:
                        ok = m if ok is None else jnp.logical_and(ok, m)
                if ok is not None:
                    tap = jnp.where(ok, tap, 0.0)
                acc = acc + tap * w_ref[kh * 3 + kw:kh * 3 + kw + 1, c0:c1]
        return acc * _sigmoid(acc)

    nq = ML_HEADS * ML_QK
    q_ref[...] = conv(qk_m, qk_p, qk_n, wq_ref, 0, nq) * (ML_QK ** -0.5)
    k_ref[...] = conv(qk_m, qk_p, qk_n, wq_ref, nq, 2 * nq)
    nh = HG_HEADS * HG_DK
    hq_ref[...] = conv(qi_m, qi_p, qi_n, wi_ref, 0, nh)
    hv_ref[...] = conv(qi_m, qi_p, qi_n, wi_ref, nh, 2 * nh)


def _conv_call(qk, qi, wq, wi, tpb):
    n = qk.shape[0]
    rpt = TILE // GRID_W
    nrow = n // GRID_W
    cq, ci = qk.shape[1], qi.shape[1]

    def main(i):
        return (i, 0)

    def prev(i):
        return (jnp.maximum(i * rpt - 1, 0), 0)

    def nxt(i):
        return (jnp.minimum((i + 1) * rpt, nrow - 1), 0)

    return pl.pallas_call(
        functools.partial(_conv_kernel, tpb=tpb),
        grid=(n // TILE,),
        in_specs=[
            pl.BlockSpec((TILE, cq), main), pl.BlockSpec((GRID_W, cq), prev), pl.BlockSpec((GRID_W, cq), nxt),
            pl.BlockSpec((TILE, ci), main), pl.BlockSpec((GRID_W, ci), prev), pl.BlockSpec((GRID_W, ci), nxt),
            pl.BlockSpec((9, cq), lambda i: (0, 0)),
            pl.BlockSpec((9, ci), lambda i: (0, 0)),
        ],
        out_specs=[pl.BlockSpec((TILE, s), main) for s in (cq // 2, cq // 2, ci // 2, ci // 2)],
        out_shape=[jax.ShapeDtypeStruct((n, s), F32) for s in (cq // 2, cq // 2, ci // 2, ci // 2)],
        compiler_params=_cparams(1),
        name="conv",
    )(qk, qk, qk, qi, qi, qi, wq.reshape(9, cq), wi.reshape(9, ci))


W_CHUNKS = TILE // CHUNK


def _tile_maps(nct, nt):
    def fwd(b, j):
        return (b * nt + j, 0)

    def bwd(b, j):
        return (b * nt + jnp.where(j < nct, nct - 1 - j, nt + nct - 1 - j), 0)

    return fwd, bwd


def _tri(fwd):
    t = lax.broadcasted_iota(jnp.int32, (CHUNK, CHUNK), 0)
    s = lax.broadcasted_iota(jnp.int32, (CHUNK, CHUNK), 1)
    return (s <= t) if fwd else (s >= t)


def _mask_dot(mask_bf, x):
    a = x.astype(BF16)
    r = x - a.astype(F32)
    b = r.astype(BF16)
    c = (r - b.astype(F32)).astype(BF16)
    return (_dot(mask_bf, a) + _dot(mask_bf, b)) + _dot(mask_bf, c)


def _mlstm_kernel(qf, kf, vf, gf, qb, kb, vb, gb, bias_ref, hf_ref, hb_ref, ct_ref, m_ref):
    @pl.when(pl.program_id(1) == 0)
    def _():
        ct_ref[...] = jnp.zeros_like(ct_ref)
        m_ref[...] = jnp.zeros_like(m_ref)

    assert CHUNK == ML_QK and 2 * ML_QK == LANES
    lane = lax.broadcasted_iota(jnp.int32, (CHUNK, LANES), 1)
    t_id = lax.broadcasted_iota(jnp.int32, (CHUNK, LANES), 0)
    lo = lane < ML_QK
    lo_row = lo[0:1, :]
    s_id = lane % ML_QK
    r128 = lax.broadcasted_iota(jnp.int32, (LANES, LANES), 0)
    c128 = lax.broadcasted_iota(jnp.int32, (LANES, LANES), 1)
    same_head = (r128 // ML_QK) == (c128 // ML_QK)
    ones_blk = jnp.ones((CHUNK, LANES), BF16)
    zeros_va = jnp.zeros((CHUNK, ML_V + LANES), BF16)

    def rep(arr, col):
        return jnp.broadcast_to(arr[:, col:col + 1], (CHUNK, LANES))

    def half_max(x, first):
        sel = lo if x.shape[0] == CHUNK else lo_row
        y = jnp.where(sel if first else jnp.logical_not(sel), x, -jnp.inf)
        return jnp.broadcast_to(jnp.max(y, axis=-1, keepdims=True), x.shape)

    for d, (q_ref, k_ref, v_ref, g_ref, h_ref) in enumerate(((qf, kf, vf, gf, hf_ref), (qb, kb, vb, gb, hb_ref))):
        fwd = d == 0
        mask_bf = jnp.where(_tri(fwd), 1.0, 0.0).astype(BF16)
        mask_p = (s_id <= t_id) if fwd else (s_id >= t_id)
        last = CHUNK - 1 if fwd else 0
        order = range(W_CHUNKS) if fwd else range(W_CHUNKS - 1, -1, -1)
        local = {}
        for c in order:
            rows = slice(c * CHUNK, (c + 1) * CHUNK)
            g_all = g_ref[rows, :] + bias_ref[...]
            bcum = _mask_dot(mask_bf, _log_sigmoid(g_all))
            g_t = g_all.T
            b_t = bcum.T
            for p in range(ML_HEADS // 2):
                ci = [d * ML_HEADS + 2 * p + e for e in (0, 1)]
                cf = [(2 + d) * ML_HEADS + 2 * p + e for e in (0, 1)]
                b_rep = [rep(bcum, cf[e]) for e in (0, 1)]
                bp = jnp.where(lo, b_rep[0], b_rep[1])
                ip = jnp.where(lo, rep(g_all, ci[0]), rep(g_all, ci[1]))
                br = jnp.concatenate([b_t[cf[0]:cf[0] + 1, :], b_t[cf[1]:cf[1] + 1, :]], axis=1)
                ir = jnp.concatenate([g_t[ci[0]:ci[0] + 1, :], g_t[ci[1]:ci[1] + 1, :]], axis=1)
                dmat = jnp.where(mask_p, bp - br + ir, -jnp.inf)
                m_rep = [half_max(dmat, True), half_max(dmat, False)]
                q01 = q_ref[rows, p * LANES:(p + 1) * LANES].astype(BF16)
                k01 = k_ref[rows, p * LANES:(p + 1) * LANES]
                k01b = k01.astype(BF16)
                k_bd = jnp.where(same_head, jnp.concatenate([k01b, k01b], axis=0), jnp.zeros((), BF16))
                s = _dot_nt(q01, k_bd) * jnp.exp(dmat - jnp.where(lo, m_rep[0], m_rep[1]))
                va = [jnp.concatenate([v_ref[rows, (2 * p + e) * ML_V:(2 * p + e + 1) * ML_V].astype(BF16), ones_blk],
                                      axis=1) for e in (0, 1)]
                v_bd = jnp.concatenate([jnp.concatenate([va[0], zeros_va], axis=1),
                                        jnp.concatenate([zeros_va, va[1]], axis=1)], axis=0)
                nd_loc = _dot(s.astype(BF16), v_bd)
                gp = bp[last:last + 1, :]
                e_row = gp - br + ir
                me_rep = [half_max(e_row, True), half_max(e_row, False)]
                kw = (k01 * jnp.exp(gp - bp + ip - jnp.where(lo_row, me_rep[0], me_rep[1]))).astype(BF16)
                upd = _dot_tn(kw, jnp.concatenate(va, axis=1))
                g_tot = [b_rep[e][last:last + 1, :] for e in (0, 1)]
                local[(c, p)] = (q01, nd_loc, upd, b_rep, m_rep, g_tot, me_rep)
        wv = ML_V + LANES
        for p in range(ML_HEADS // 2):
            ci = [d * ML_HEADS + 2 * p + e for e in (0, 1)]
            m = [m_ref[ci[e]:ci[e] + 1, :] for e in (0, 1)]
            ct = [ct_ref[ci[e]] for e in (0, 1)]
            for c in order:
                q01, nd_loc, upd, b_rep, m_rep, g_tot, me_rep = local[(c, p)]
                ct_both = jnp.concatenate(ct, axis=0).astype(BF16)
                for e in (0, 1):
                    h = 2 * p + e
                    q_e = jnp.where(lo if e == 0 else jnp.logical_not(lo), q01, jnp.zeros((), BF16))
                    qc = _dot(q_e, ct_both)
                    a_inter = b_rep[e] + m[e]
                    m_t = jnp.maximum(a_inter, m_rep[e])
                    w_loc = jnp.exp(m_rep[e] - m_t)
                    w_int = jnp.exp(a_inter - m_t)
                    num = w_loc * nd_loc[:, e * wv:e * wv + ML_V] + w_int * qc[:, 0:ML_V]
                    den = w_loc * nd_loc[:, e * wv + ML_V:(e + 1) * wv] + w_int * qc[:, ML_V:wv]
                    h_ref[c * CHUNK:(c + 1) * CHUNK, h * ML_V:(h + 1) * ML_V] = (
                        num / jnp.maximum(jnp.abs(den), jnp.exp(-m_t)))
                    m_new = jnp.maximum(g_tot[e] + m[e], me_rep[e])
                    w_c = jnp.exp(g_tot[e] + m[e] - m_new)
                    w_u = jnp.exp(me_rep[e] - m_new)
                    ct[e] = (jnp.concatenate([w_c, w_c], axis=1) * ct[e]
                             + jnp.concatenate([w_u, w_u], axis=1) * upd[e * ML_QK:(e + 1) * ML_QK, e * wv:(e + 1) * wv])
                    m[e] = m_new
            for e in (0, 1):
                ct_ref[ci[e]] = ct[e]
                m_ref[ci[e]:ci[e] + 1, :] = m[e]


def _mlstm_call(q, k, v, gates, gate_b, n_batch, nct):
    n = q.shape[0]
    nt = n // TILE // n_batch
    fwd, bwd = _tile_maps(nct, nt)
    dq, dv = q.shape[1], v.shape[1]
    specs = lambda m: [pl.BlockSpec((TILE, dq), m), pl.BlockSpec((TILE, dq), m),
                       pl.BlockSpec((TILE, dv), m), pl.BlockSpec((TILE, LANES), m)]
    bias = jnp.zeros((1, LANES), F32).at[0, :4 * ML_HEADS].set(gate_b.reshape(-1))
    return pl.pallas_call(
        _mlstm_kernel,
        grid=(n_batch, nt),
        in_specs=specs(fwd) + specs(bwd) + [pl.BlockSpec((1, LANES), lambda b, j: (0, 0))],
        out_specs=[pl.BlockSpec((TILE, dv), fwd), pl.BlockSpec((TILE, dv), bwd)],
        out_shape=[jax.ShapeDtypeStruct((n, dv), F32)] * 2,
        scratch_shapes=[pltpu.VMEM((2 * ML_HEADS, ML_QK, ML_V + LANES), F32), pltpu.VMEM((2 * ML_HEADS, LANES), F32)],
        compiler_params=_cparams(2),
        name="mlstm",
    )(q, k, v, gates, q, k, v, gates, bias)


def _hgrn_kernel(qf, vf, ff, qb, vb, fb, fbias_ref, lb_ref, of_ref, ob_ref, st_ref):
    @pl.when(pl.program_id(1) == 0)
    def _():
        st_ref[...] = jnp.zeros_like(st_ref)

    n_sub = CHUNK // SUB
    rows = lax.broadcasted_iota(jnp.int32, (CHUNK, 1), 0)
    t_i = lax.broadcasted_iota(jnp.int32, (CHUNK, CHUNK), 0)
    s_i = lax.broadcasted_iota(jnp.int32, (CHUNK, CHUNK), 1)
    same_sub = (t_i // SUB) == (s_i // SUB)
    lb = lb_ref[...]
    for d, (q_ref, v_ref, f_ref, o_ref) in enumerate(((qf, vf, ff, of_ref), (qb, vb, fb, ob_ref))):
        fwd = d == 0
        mask = _tri(fwd)
        mask_bf = jnp.where(mask, 1.0, 0.0).astype(BF16)
        diag_ok = jnp.logical_and(same_sub, mask)
        last = CHUNK - 1 if fwd else 0
        order = range(W_CHUNKS) if fwd else range(W_CHUNKS - 1, -1, -1)
        local = {}
        for c in order:
            rs = slice(c * CHUNK, (c + 1) * CHUNK)
            f = lb + (1.0 - lb) * _sigmoid(f_ref[rs, :] + fbias_ref[d:d + 1, :])
            kk = 1.0 - f
            b = _mask_dot(mask_bf, jnp.log(f))
            q = q_ref[rs, :]
            q_parts, k_parts = [], []
            for p in range(1, n_sub):
                if fwd:
                    ref_row, q_ok, k_ok = p * SUB - 1, (rows // SUB) == p, rows < p * SUB
                else:
                    ref_row, q_ok, k_ok = p * SUB, (rows // SUB) == p - 1, rows >= p * SUB
                r_p = b[ref_row:ref_row + 1, :]
                q_parts.append(jnp.where(q_ok, q * jnp.exp(jnp.minimum(b - r_p, 0.0)), 0.0))
                k_parts.append(jnp.where(k_ok, kk * jnp.exp(jnp.minimum(r_p - b, 0.0)), 0.0))
            mid = SUB // 2 - 1 if fwd else SUB // 2
            c_m = b[mid:mid + 1, :]
            for i in range(1, n_sub):
                c_m = jnp.where(rows >= i * SUB, b[i * SUB + mid:i * SUB + mid + 1, :], c_m)
            q_d = q * jnp.exp(b - c_m)
            k_d = kk * jnp.exp(c_m - b)
            g_row = b[last:last + 1, :]
            q_in = (q * jnp.exp(b)).astype(BF16)
            k_out = (kk * jnp.exp(g_row - b)).astype(BF16)
            decay = jnp.exp(g_row)
            for h in range(HG_HEADS):
                sl = slice(h * HG_DK, (h + 1) * HG_DK)
                qc = jnp.concatenate([x[:, sl] for x in q_parts], axis=1).astype(BF16)
                kc = jnp.concatenate([x[:, sl] for x in k_parts], axis=1).astype(BF16)
                a = _dot_nt(qc, kc) + jnp.where(diag_ok, _dot_nt(q_d[:, sl].astype(BF16), k_d[:, sl].astype(BF16)), 0.0)
                vh = v_ref[rs, sl].astype(BF16)
                local[(c, h)] = (_dot(a.astype(BF16), vh), q_in[:, sl], decay[:, sl], _dot_tn(vh, k_out[:, sl]))
        for h in range(HG_HEADS):
            st = st_ref[d * HG_HEADS + h]
            for c in order:
                o_loc, q_in, decay, upd = local[(c, h)]
                o_ref[c * CHUNK:(c + 1) * CHUNK, h * HG_DK:(h + 1) * HG_DK] = o_loc + _dot_nt(q_in, st.astype(BF16))
                st = st * decay + upd
            st_ref[d * HG_HEADS + h] = st


def _hgrn_call(hq, hv, ff, fb, f_b, lb, n_batch, nct):
    n, w = hq.shape
    nt = n // TILE // n_batch
    fwd, bwd = _tile_maps(nct, nt)
    spec = lambda m: pl.BlockSpec((TILE, w), m)
    return pl.pallas_call(
        _hgrn_kernel,
        grid=(n_batch, nt),
        in_specs=[spec(fwd)] * 3 + [spec(bwd)] * 3 + [pl.BlockSpec((2, w), lambda b, j: (0, 0)),
                                                      pl.BlockSpec((1, w), lambda b, j: (0, 0))],
        out_specs=[spec(fwd), spec(bwd)],
        out_shape=[jax.ShapeDtypeStruct((n, w), F32)] * 2,
        scratch_shapes=[pltpu.VMEM((2 * HG_HEADS, HG_DK, HG_DK), F32)],
        compiler_params=_cparams(2),
        name="hgrn",
    )(hq, hv, ff, hq, hv, fb, f_b, lb.reshape(1, w))


def _head_rms(h, g, width):
    parts = []
    for i in range(h.shape[1] // width):
        p = h[:, i * width:(i + 1) * width]
        parts.append(p * lax.rsqrt(jnp.mean(p * p, axis=-1, keepdims=True) + EPS))
    return jnp.concatenate(parts, axis=1) * g


def _out_kernel(x_ref, mlf, mlb, hgf, hgb, o_ref, go_ref, mod_ref, gml_ref, ghg_ref, wout_ref, g2_ref, rw_ref, rb_ref,
                xo_ref, h2_ref, idx_ref, gate_ref, *, tpb, n_batch):
    d = x_ref.shape[1]
    for sub in range(OUT_TILES):
        rs = slice(sub * TILE, (sub + 1) * TILE)
        mod = _mod_row(mod_ref, tpb, n_batch, pl.program_id(0) * OUT_TILES + sub)
        y_ml = _sigmoid(o_ref[rs, :]) * _head_rms(mlf[rs, :] + mlb[rs, :], gml_ref[...], ML_V)
        go = go_ref[rs, :]
        y_hg = go * _sigmoid(go) * _head_rms(hgf[rs, :] + hgb[rs, :], ghg_ref[...], HG_DK)
        y = jnp.concatenate([y_ml, y_hg], axis=1).astype(BF16)
        x = x_ref[rs, :] + mod[:, 2 * d:3 * d] * _dot(y, wout_ref[...])
        xo_ref[rs, :] = x
        h2 = _rms(x, g2_ref[...]) * (1.0 + mod[:, 4 * d:5 * d]) + mod[:, 3 * d:4 * d]
        _store_slabs(h2_ref, h2, sub * TILE * SLAB)
        h_hi = h2.astype(BF16)
        r1 = h2 - h_hi.astype(F32)
        h_mid = r1.astype(BF16)
        h_lo = (r1 - h_mid.astype(F32)).astype(BF16)
        p_hi = _dot(h_hi, rw_ref[...])
        p_mid = _dot(h_mid, rw_ref[:, :2 * LANES])
        p_lo = _dot(h_lo, rw_ref[:, :LANES])
        small = (p_lo + p_mid[:, LANES:]) + p_hi[:, 2 * LANES:]
        vals = ((small + (p_mid[:, :LANES] + p_hi[:, LANES:2 * LANES])) + p_hi[:, :LANES]) + rb_ref[...]
        lane = lax.broadcasted_iota(jnp.int32, vals.shape, 1)
        lane_f = lane.astype(F32)
        idx_out = jnp.zeros(vals.shape, F32)
        top = []
        for k in range(TOP_K):
            mx = jnp.max(vals, axis=-1, keepdims=True)
            ix = jnp.min(jnp.where(vals == mx, lane_f, float(LANES)), axis=-1, keepdims=True)
            top.append(mx)
            idx_out = jnp.where(lane == k, ix, idx_out)
            vals = jnp.where(lane_f == ix, -jnp.inf, vals)
        ex = [jnp.exp(t - top[0]) for t in top]
        tot = ex[0] + ex[1] + ex[2] + ex[3]
        gate_out = jnp.zeros(vals.shape, F32)
        for k in range(TOP_K):
            gate_out = jnp.where(lane == k, ex[k] / tot, gate_out)
        idx_ref[rs, :] = idx_out.astype(jnp.int32)
        gate_ref[rs, :] = gate_out


def _out_call(x, mlf, mlb, hgf, hgb, o, go, mod_l, gml, ghg, wout, g2, rw, rb, tpb, n_batch):
    n, d = x.shape
    w = mlf.shape[1]
    row = lambda i: (i, 0)
    fix = lambda i: (0, 0)
    rw_p = jnp.zeros((d, LANES), F32).at[:, :N_EXPERTS].set(rw)
    rw_hi = rw_p.astype(BF16)
    rw_r1 = rw_p - rw_hi.astype(F32)
    rw_mid = rw_r1.astype(BF16)
    rw_cat = jnp.concatenate([rw_hi, rw_mid, (rw_r1 - rw_mid.astype(F32)).astype(BF16)], axis=1)
    rb_p = jnp.full((1, LANES), NEG, F32).at[0, :N_EXPERTS].set(rb)
    rows = OUT_TILES * TILE
    assert n % rows == 0
    return pl.pallas_call(
        functools.partial(_out_kernel, tpb=tpb, n_batch=n_batch),
        grid=(n // rows,),
        in_specs=[pl.BlockSpec((rows, d), row)] + [pl.BlockSpec((rows, w), row)] * 6 + [
            pl.BlockSpec((8, 6 * d), fix), pl.BlockSpec((1, w), fix), pl.BlockSpec((1, w), fix),
            pl.BlockSpec((d, d), fix), pl.BlockSpec((1, d), fix), pl.BlockSpec((d, 3 * LANES), fix),
            pl.BlockSpec((1, LANES), fix)],
        out_specs=[pl.BlockSpec((rows, d), row), pl.BlockSpec((rows * SLAB, LANES), row),
                   pl.BlockSpec((rows, LANES), row), pl.BlockSpec((rows, LANES), row)],
        out_shape=[jax.ShapeDtypeStruct((n, d), F32), jax.ShapeDtypeStruct((n * SLAB, LANES), F32),
                   jax.ShapeDtypeStruct((n, LANES), jnp.int32), jax.ShapeDtypeStruct((n, LANES), F32)],
        compiler_params=_cparams(1),
        name="outproj",
    )(x, mlf, mlb, hgf, hgb, o, go, mod_l, gml.reshape(1, w), ghg.reshape(1, w), wout, g2.reshape(1, d), rw_cat, rb_p)


def _onehots(idx):
    lane = lax.broadcasted_iota(jnp.int32, idx.shape, 1)
    return [lane == idx[:, k:k + 1] for k in range(TOP_K)]


def _rank_kernel(idx_ref, rank_ref, cnt_ref, carry_ref):
    @pl.when(pl.program_id(0) == 0)
    def _():
        carry_ref[...] = jnp.zeros_like(carry_ref)

    hots = _onehots(idx_ref[...])
    m = jnp.zeros(idx_ref.shape, F32)
    for hk in hots:
        m = m + jnp.where(hk, 1.0, 0.0)
    t_i = lax.broadcasted_iota(jnp.int32, (TILE, TILE), 0)
    s_i = lax.broadcasted_iota(jnp.int32, (TILE, TILE), 1)
    before = _dot((s_i < t_i).astype(BF16), m.astype(BF16)) + carry_ref[...]
    lane = lax.broadcasted_iota(jnp.int32, idx_ref.shape, 1)
    out = jnp.zeros(idx_ref.shape, F32)
    for k, hk in enumerate(hots):
        rk = jnp.sum(jnp.where(hk, before, 0.0), axis=-1, keepdims=True)
        out = jnp.where(lane == k, rk, out)
    rank_ref[...] = out.astype(jnp.int32)
    carry_ref[...] = carry_ref[...] + jnp.sum(m, axis=0, keepdims=True)
    cnt_ref[...] = carry_ref[...]


def _rank_call(idx):
    n = idx.shape[0]
    return pl.pallas_call(
        _rank_kernel,
        grid=(n // TILE,),
        in_specs=[pl.BlockSpec((TILE, LANES), lambda i: (i, 0))],
        out_specs=[pl.BlockSpec((TILE, LANES), lambda i: (i, 0)), pl.BlockSpec((1, LANES), lambda i: (0, 0))],
        out_shape=[jax.ShapeDtypeStruct((n, LANES), jnp.int32), jax.ShapeDtypeStruct((1, LANES), F32)],
        scratch_shapes=[pltpu.VMEM((1, LANES), F32)],
        compiler_params=_cparams(1),
        name="rank",
    )(idx)


def _dest_kernel(idx_ref, rank_ref, cnt_ref, dest_ref, meta_ref, *, n_blocks):
    cnt = cnt_ref[...]
    padded = jnp.floor((cnt + (MOE_BLK - 1)) * (1.0 / MOE_BLK)) * MOE_BLK
    e_i = lax.broadcasted_iota(jnp.int32, (LANES, LANES), 0)
    e_j = lax.broadcasted_iota(jnp.int32, (LANES, LANES), 1)
    pad_start = _dot_hi(jnp.broadcast_to(padded, (8, LANES)), (e_i < e_j).astype(F32))[0:1]
    pad_end = pad_start + padded
    lane = lax.broadcasted_iota(jnp.int32, idx_ref.shape, 1)
    rank = rank_ref[...]
    out = jnp.zeros(idx_ref.shape, jnp.int32)
    for k, hk in enumerate(_onehots(idx_ref[...])):
        st = jnp.sum(jnp.where(hk, pad_start, 0.0), axis=-1, keepdims=True).astype(jnp.int32)
        out = jnp.where(lane == k, st + rank[:, k:k + 1], out)
    dest_ref[...] = out
    blk = lax.broadcasted_iota(jnp.int32, meta_ref.shape, 0).astype(F32) * MOE_BLK
    lane_m = lax.broadcasted_iota(jnp.int32, meta_ref.shape, 1)
    done = jnp.where(jnp.logical_and(lane_m < N_EXPERTS, pad_end <= blk), 1.0, 0.0)
    be = jnp.minimum(jnp.sum(done, axis=-1, keepdims=True), N_EXPERTS - 1.0)
    used = jnp.sum(jnp.where(lane_m[0:1] < N_EXPERTS, padded, 0.0), axis=-1, keepdims=True) * (1.0 / MOE_BLK)
    diag = lane_m == lax.broadcasted_iota(jnp.int32, meta_ref.shape, 0)
    end_col = jnp.sum(jnp.where(diag, pad_end, 0.0), axis=-1, keepdims=True)
    pad_col = jnp.sum(jnp.where(diag, padded, 0.0), axis=-1, keepdims=True)
    meta = jnp.where(lane_m == 0, be, jnp.where(lane_m == 1, used, jnp.where(lane_m == 2, end_col, pad_col)))
    meta_ref[...] = jnp.where(lane_m < 4, meta, 0.0).astype(jnp.int32)


def _dest_call(idx, rank, cnt, n_blocks):
    n = idx.shape[0]
    nb_pad = -(-n_blocks // 8) * 8
    row = lambda i: (i, 0)
    return pl.pallas_call(
        functools.partial(_dest_kernel, n_blocks=n_blocks),
        grid=(n // TILE,),
        in_specs=[pl.BlockSpec((TILE, LANES), row), pl.BlockSpec((TILE, LANES), row),
                  pl.BlockSpec((1, LANES), lambda i: (0, 0))],
        out_specs=[pl.BlockSpec((TILE, LANES), row), pl.BlockSpec((nb_pad, LANES), lambda i: (0, 0))],
        out_shape=[jax.ShapeDtypeStruct((n, LANES), jnp.int32), jax.ShapeDtypeStruct((nb_pad, LANES), jnp.int32)],
        compiler_params=_cparams(1),
        name="dest",
    )(idx, rank, cnt)


ROW_UNROLL = 8


def _scatter_kernel(dest_ref, zinfo_ref, h_ref, xb_ref, zbuf, sem):
    blk_rows = MOE_BLK * SLAB

    @pl.when(pl.program_id(0) == 0)
    def _():
        zbuf[...] = jnp.zeros_like(zbuf)
        n_blocks = xb_ref.shape[0] // blk_rows
        used = zinfo_ref[2 * N_EXPERTS]
        for stage in ("start", "wait"):
            for e in range(N_EXPERTS):
                @pl.when(zinfo_ref[2 * e + 1] > 0)
                def _():
                    first = pl.multiple_of((zinfo_ref[2 * e] - MOE_BLK) * SLAB, blk_rows)
                    cp = pltpu.make_async_copy(zbuf, xb_ref.at[pl.ds(first, blk_rows), :], sem)
                    cp.start() if stage == "start" else cp.wait()

                @pl.when(used + e < n_blocks)
                def _():
                    first = pl.multiple_of((used + e) * blk_rows, blk_rows)
                    cp = pltpu.make_async_copy(zbuf, xb_ref.at[pl.ds(first, blk_rows), :], sem)
                    cp.start() if stage == "start" else cp.wait()

    def issue(g, c):
        for u in range(ROW_UNROLL):
            src = h_ref.at[pl.ds(pl.multiple_of((g * ROW_UNROLL + u) * SLAB, SLAB), SLAB), :]
            for k in range(TOP_K):
                dst = pl.multiple_of(dest_ref[g * (ROW_UNROLL * TOP_K) + u * TOP_K + k] * SLAB, SLAB)
                pltpu.make_async_copy(src, xb_ref.at[pl.ds(dst, SLAB), :], sem).start(priority=k % 2)
        return c

    lax.fori_loop(0, TILE // ROW_UNROLL, issue, 0)
    all_rows = xb_ref.at[pl.ds(0, TILE * TOP_K * SLAB), :]
    pltpu.make_async_copy(all_rows, all_rows, sem).wait()


def _scatter_call(dest_flat, zinfo, h2s, n_rows):
    n = h2s.shape[0] // SLAB
    return pl.pallas_call(
        _scatter_kernel,
        grid=(n // TILE,),
        in_specs=[pl.BlockSpec((TILE * TOP_K,), lambda i: (i,), memory_space=pltpu.SMEM),
                  pl.BlockSpec(memory_space=pltpu.SMEM),
                  pl.BlockSpec((TILE * SLAB, LANES), lambda i: (i, 0))],
        out_specs=pl.BlockSpec(memory_space=pl.ANY),
        out_shape=jax.ShapeDtypeStruct((n_rows * SLAB, LANES), F32),
        scratch_shapes=[pltpu.VMEM((MOE_BLK * SLAB, LANES), F32), pltpu.SemaphoreType.DMA(())],
        compiler_params=_cparams(1),
        name="scatter",
    )(dest_flat, zinfo, h2s)


CAST_ROWS = 64


def _cast_rows(src_ref, dst_ref):
    def body(r, c):
        rows = pl.ds(pl.multiple_of(r * CAST_ROWS, CAST_ROWS), CAST_ROWS)
        dst_ref[rows, :] = src_ref[rows, :].astype(BF16)
        return c

    lax.fori_loop(0, src_ref.shape[0] // CAST_ROWS, body, 0)


def _expert_kernel(be_ref, used_ref, nblk_ref, x_ref, wgu_hbm, bgu_ref, wd_hbm, bd_ref, y_ref,
                   wgu_f, wd_f, wgu_bf, wd_bf, slot_ref, sem, *, layer):
    i = pl.program_id(0)
    used = used_ref[0]
    live = i < used
    e = be_ref[i]
    new_expert = jnp.logical_or(i == 0, e != be_ref[jnp.maximum(i - 1, 0)])

    def fetch(expert, slot):
        return (pltpu.make_async_copy(wgu_hbm.at[layer, expert], wgu_f.at[slot], sem.at[0, slot]),
                pltpu.make_async_copy(wd_hbm.at[layer, expert], wd_f.at[slot], sem.at[1, slot]))

    @pl.when(jnp.logical_and(live, i == 0))
    def _():
        slot_ref[0] = 0
        for cp in fetch(e, 0):
            cp.start()

    @pl.when(jnp.logical_not(live))
    def _():
        y_ref[...] = jnp.zeros_like(y_ref)

    @pl.when(jnp.logical_and(live, new_expert))
    def _():
        slot = slot_ref[0]
        for cp in fetch(e, slot):
            cp.wait()
        nxt = i + nblk_ref[e]

        @pl.when(nxt < used)
        def _():
            for cp in fetch(be_ref[nxt], 1 - slot):
                cp.start()

        _cast_rows(wgu_f.at[slot], wgu_bf)
        _cast_rows(wd_f.at[slot], wd_bf)
        slot_ref[0] = 1 - slot

    @pl.when(live)
    def _():
        de = wd_bf.shape[0]
        gu = _dot(_load_slabs(x_ref, MOE_BLK).astype(BF16), wgu_bf[...]) + bgu_ref[...]
        glu = jnp.minimum(gu[:, :de], SWIGLU_LIMIT)
        lin = jnp.clip(gu[:, de:], -SWIGLU_LIMIT, SWIGLU_LIMIT)
        act = glu * _sigmoid(SWIGLU_ALPHA * glu) * (lin + 1.0)
        _store_slabs(y_ref, _dot(act.astype(BF16), wd_bf[...]) + bd_ref[...])


def _expert_call(layer, be, used, nblk, xb, wgu, bgu, wd, bd):
    n_rows = xb.shape[0] // SLAB
    depth, e, d, de2 = wgu.shape
    nb = n_rows // MOE_BLK
    last_live = lambda i, used: jnp.maximum(jnp.minimum(i, used[0] - 1), 0)
    blk = lambda i, be, used, nblk: (last_live(i, used), 0)
    bsel = lambda i, be, used, nblk: (layer, be[last_live(i, used)], 0, 0)
    return pl.pallas_call(
        functools.partial(_expert_kernel, layer=layer),
        grid_spec=pltpu.PrefetchScalarGridSpec(
            num_scalar_prefetch=3,
            grid=(nb,),
            in_specs=[pl.BlockSpec((MOE_BLK * SLAB, LANES), blk),
                      pl.BlockSpec(memory_space=pl.ANY), pl.BlockSpec((None, None, 1, de2), bsel),
                      pl.BlockSpec(memory_space=pl.ANY), pl.BlockSpec((None, None, 1, d), bsel)],
            out_specs=pl.BlockSpec((MOE_BLK * SLAB, LANES), lambda i, be, used, nblk: (i, 0)),
            scratch_shapes=[pltpu.VMEM((2, d, de2), F32), pltpu.VMEM((2, de2 // 2, d), F32),
                            pltpu.VMEM((d, de2), BF16), pltpu.VMEM((de2 // 2, d), BF16),
                            pltpu.SMEM((1,), jnp.int32), pltpu.SemaphoreType.DMA((2, 2))],
        ),
        out_shape=jax.ShapeDtypeStruct((n_rows * SLAB, LANES), F32),
        compiler_params=_cparams(1, vmem=EXPERT_VMEM_LIMIT),
        name="expert",
    )(be, used, nblk, xb, wgu, bgu.reshape(depth, e, 1, de2), wd, bd.reshape(depth, e, 1, d))


def _combine_kernel(dest_ref, x_ref, gate_ref, mod_ref, fg_ref, yb_ref, o_ref, buf, sem, *, tpb, n_batch, final):
    def issue(g, c):
        for u in range(ROW_UNROLL):
            rows = pl.ds(pl.multiple_of((g * ROW_UNROLL + u) * SLAB, SLAB), SLAB)
            for k in range(TOP_K):
                src = pl.multiple_of(dest_ref[g * (ROW_UNROLL * TOP_K) + u * TOP_K + k] * SLAB, SLAB)
                pltpu.make_async_copy(yb_ref.at[pl.ds(src, SLAB), :], buf.at[k, rows, :], sem).start(priority=k % 2)
        return c

    lax.fori_loop(0, TILE // ROW_UNROLL, issue, 0)
    pltpu.make_async_copy(buf, buf, sem).wait()
    d = x_ref.shape[1]
    mod = _mod_row(mod_ref, tpb, n_batch)
    gates = gate_ref[...]
    f = gates[:, 0:1] * _load_slabs(buf.at[0], TILE)
    for k in range(1, TOP_K):
        f = f + gates[:, k:k + 1] * _load_slabs(buf.at[k], TILE)
    x = x_ref[...] + mod[:, 5 * d:6 * d] * f
    o_ref[...] = _rms(x, fg_ref[...]) if final else x


def _combine_call(dest_flat, x, gates, mod_l, fg, yb, tpb, n_batch, final):
    n, d = x.shape
    if final:
        out_rows = n - n_batch * TILE
        omap = lambda i: ((i // tpb) * (tpb - 1) + jnp.maximum(i % tpb - 1, 0), 0)
    else:
        out_rows = n
        omap = lambda i: (i, 0)
    return pl.pallas_call(
        functools.partial(_combine_kernel, tpb=tpb, n_batch=n_batch, final=final),
        grid=(n // TILE,),
        in_specs=[pl.BlockSpec((TILE * TOP_K,), lambda i: (i,), memory_space=pltpu.SMEM),
                  pl.BlockSpec((TILE, d), lambda i: (i, 0)),
                  pl.BlockSpec((TILE, LANES), lambda i: (i, 0)),
                  pl.BlockSpec((8, 6 * d), lambda i: (0, 0)),
                  pl.BlockSpec((1, d), lambda i: (0, 0)),
                  pl.BlockSpec(memory_space=pl.ANY)],
        out_specs=pl.BlockSpec((TILE, d), omap),
        out_shape=jax.ShapeDtypeStruct((out_rows, d), F32),
        scratch_shapes=[pltpu.VMEM((TOP_K, TILE * SLAB, LANES), F32), pltpu.SemaphoreType.DMA(())],
        compiler_params=_cparams(1),
        name="combine",
    )(dest_flat, x, gates, mod_l, fg.reshape(1, d), yb)


def kernel(x, c, ctx, c_ctx, w_ada, b_ada, norm1_g, w_in, mlstm_conv, mlstm_gate_b, mlstm_norm_g, hgrn_conv, hgrn_f_b,
           hgrn_lb_raw, hgrn_norm_g, w_out, norm2_g, router_w, router_b, w_gu, b_gu, w_down, b_down, final_g):
    n_batch, seq, d = x.shape
    ctx_len = ctx.shape[1]
    depth = w_ada.shape[0]
    assert ctx_len == TILE and seq % TILE == 0 and n_batch + 1 <= 8 and d == SLAB * LANES
    tpb = (ctx_len + seq) // TILE
    nct = ctx_len // TILE
    n = n_batch * (ctx_len + seq)
    n_blocks = -(-(n * TOP_K) // MOE_BLK) + N_EXPERTS
    n_rows = n_blocks * MOE_BLK

    xa = jnp.concatenate([ctx, x], axis=1).reshape(n, d)
    cond = jnp.zeros((8, d), F32).at[:n_batch].set(c).at[n_batch].set(c_ctx)
    mod = _ada_call(cond, w_ada, b_ada)

    lb_w = jax.nn.softmax(hgrn_lb_raw.astype(F32), axis=0)
    lower = jnp.cumsum(lb_w, axis=0) - lb_w[0]

    n_gate = 4 * ML_HEADS
    g0 = 3 * 512
    w_in_p = jnp.concatenate([w_in[:, :, :g0], jnp.pad(w_in[:, :, g0:g0 + n_gate], ((0, 0), (0, 0), (0, LANES - n_gate))),
                              w_in[:, :, g0 + n_gate:]], axis=2).astype(BF16)
    w_out_b = w_out.astype(BF16)

    for l in range(depth):
        last = l == depth - 1
        qk, v, o, gates, qi, ff, fb, go = _inproj_call(xa, mod[l], norm1_g[l], w_in_p[l], tpb, n_batch)
        q, k, hq, hv = _conv_call(qk, qi, mlstm_conv[l], hgrn_conv[l], tpb)
        mlf, mlb = _mlstm_call(q, k, v, gates, mlstm_gate_b[l], n_batch, nct)
        hgf, hgb = _hgrn_call(hq, hv, ff, fb, hgrn_f_b[l], lower[l], n_batch, nct)
        xa, h2, idx, gate = _out_call(xa, mlf, mlb, hgf, hgb, o, go, mod[l], mlstm_norm_g[l], hgrn_norm_g[l], w_out_b[l],
                                      norm2_g[l], router_w[l], router_b[l], tpb, n_batch)
        rank, cnt = _rank_call(idx)
        dest, meta = _dest_call(idx, rank, cnt, n_blocks)
        dest_flat = dest[:, :TOP_K].reshape(-1)
        zinfo = jnp.concatenate([meta[:N_EXPERTS, 2:4].reshape(-1), meta[0:1, 1]])
        xb = _scatter_call(dest_flat, zinfo, h2, n_rows)
        nblk = meta[:N_EXPERTS, 3] // MOE_BLK
        yb = _expert_call(l, meta[:n_blocks, 0], meta[0:1, 1], nblk, xb, w_gu, b_gu, w_down, b_down)
        xa = _combine_call(dest_flat, xa, gate, mod[l], final_g, yb, tpb, n_batch, last)
    return xa.reshape(n_batch, seq, d)
```

```python
import functools

import jax
import jax.numpy as jnp
from jax import lax
from jax.experimental import pallas as pl
from jax.experimental.pallas import tpu as pltpu

F32 = jnp.float32
BF16 = jnp.bfloat16
HIGHEST = lax.Precision.HIGHEST

GRID_W = 64
CHUNK = 64
ML_HEADS = 4
ML_QK = 64
ML_V = 128
HG_HEADS = 4
HG_DK = 128
N_EXPERTS = 32
TOP_K = 4
SWIGLU_LIMIT = 7.0
SWIGLU_ALPHA = 1.702
EPS = 1e-6

TILE = 256
OUT_TILES = 2
OUT_ROWS = 128
MOE_BLK = 256
LANES = 128
SUB = 16
NEG = -1e30
VMEM_LIMIT = 48 * 1024 * 1024
EXPERT_VMEM_LIMIT = 56 * 1024 * 1024


def _cparams(n_axes, vmem=VMEM_LIMIT):
    return pltpu.CompilerParams(dimension_semantics=("arbitrary",) * n_axes, vmem_limit_bytes=vmem)


def _dot(a, b):
    return jnp.dot(a, b, preferred_element_type=F32)


def _dot_nt(a, b):
    return lax.dot_general(a, b, (((1,), (1,)), ((), ())), preferred_element_type=F32)


def _dot_tn(a, b):
    return lax.dot_general(a, b, (((0,), (0,)), ((), ())), preferred_element_type=F32)


def _dot_hi(a, b):
    return jnp.dot(a, b, precision=HIGHEST, preferred_element_type=F32)


def _rms(x, g):
    return x * lax.rsqrt(jnp.mean(x * x, axis=-1, keepdims=True) + EPS) * g


def _sigmoid(x):
    return 1.0 / (1.0 + jnp.exp(-x))


def _log_sigmoid(x):
    return jnp.minimum(x, 0.0) - jnp.log(1.0 + jnp.exp(-jnp.abs(x)))


SLAB = 8


def _store_slabs(ref, x, base=0):
    rows = x.shape[0]
    for s in range(SLAB):
        ref[pl.ds(base + s, rows, stride=SLAB), :] = x[:, s * LANES:(s + 1) * LANES]


def _load_slabs(ref, rows):
    return jnp.concatenate([ref[pl.ds(s, rows, stride=SLAB), :] for s in range(SLAB)], axis=1)


def _ada_kernel(cond_ref, w_ref, b_ref, o_ref):
    c = cond_ref[...]
    o_ref[...] = _dot_hi(c * _sigmoid(c), w_ref[...]) + b_ref[...]


def _ada_call(cond, w_ada, b_ada):
    depth, d, n6 = w_ada.shape
    tn = 1024
    return pl.pallas_call(
        _ada_kernel,
        grid=(depth, n6 // tn),
        in_specs=[
            pl.BlockSpec((8, d), lambda l, j: (0, 0)),
            pl.BlockSpec((None, d, tn), lambda l, j: (l, 0, j)),
            pl.BlockSpec((None, 1, tn), lambda l, j: (l, 0, j)),
        ],
        out_specs=pl.BlockSpec((None, 8, tn), lambda l, j: (l, 0, j)),
        out_shape=jax.ShapeDtypeStruct((depth, 8, n6), F32),
        compiler_params=_cparams(2),
        name="ada",
    )(cond, w_ada, b_ada.reshape(depth, 1, n6))


def _mod_row(mod_ref, tpb, n_batch, tile=None):
    i = pl.program_id(0) if tile is None else tile
    row = jnp.where(i % tpb == 0, n_batch, i // tpb)
    return mod_ref[pl.ds(row, 1), :]


IN_SEGS = (512, 512, 512, 128, 1024, 512, 512, 512)


def _inproj_kernel(x_ref, mod_ref, g_ref, w_ref, *outs, tpb, n_batch):
    d = x_ref.shape[1]
    mod = _mod_row(mod_ref, tpb, n_batch)
    h = _rms(x_ref[...], g_ref[...]) * (1.0 + mod[:, d:2 * d]) + mod[:, 0:d]
    hb = h.astype(BF16)
    off = 0
    for o_ref, n in zip(outs, IN_SEGS):
        o_ref[...] = _dot(hb, w_ref[:, off:off + n])
        off += n


def _inproj_call(x, mod_l, g, w_p, tpb, n_batch):
    n, d = x.shape
    nw = w_p.shape[1]
    return pl.pallas_call(
        functools.partial(_inproj_kernel, tpb=tpb, n_batch=n_batch),
        grid=(n // TILE,),
        in_specs=[
            pl.BlockSpec((TILE, d), lambda i: (i, 0)),
            pl.BlockSpec((8, 6 * d), lambda i: (0, 0)),
            pl.BlockSpec((1, d), lambda i: (0, 0)),
            pl.BlockSpec((d, nw), lambda i: (0, 0)),
        ],
        out_specs=[pl.BlockSpec((TILE, s), lambda i: (i, 0)) for s in IN_SEGS],
        out_shape=[jax.ShapeDtypeStruct((n, s), F32) for s in IN_SEGS],
        compiler_params=_cparams(1),
        name="inproj",
    )(x, mod_l, g.reshape(1, d), w_p)


def _conv_kernel(qk_m, qk_p, qk_n, qi_m, qi_p, qi_n, wq_ref, wi_ref, q_ref, k_ref, hq_ref, hv_ref, *, tpb):
    r = lax.broadcasted_iota(jnp.int32, (TILE, 1), 0)
    j = jnp.zeros((TILE, 1), jnp.int32) + pl.program_id(0) % tpb
    is_ctx = j == 0
    col = r % GRID_W
    ok_l = jnp.logical_or(col != 0, jnp.logical_and(is_ctx, r != 0))
    ok_r = jnp.logical_or(col != GRID_W - 1, jnp.logical_and(is_ctx, r != TILE - 1))
    ok_u = jnp.logical_and(jnp.logical_not(is_ctx), jnp.logical_or(j != 1, r >= GRID_W))
    ok_d = jnp.logical_and(jnp.logical_not(is_ctx), jnp.logical_or(j != tpb - 1, r < TILE - GRID_W))
    col_ok = (ok_l, None, ok_r)
    row_ok = (ok_u, None, ok_d)
    n_ext = TILE + 2 * GRID_W

    def conv(main, prev, nxt, w_ref, c0, c1):
        ext = jnp.concatenate([prev[:, c0:c1], main[:, c0:c1], nxt[:, c0:c1]], axis=0)
        shifted = (pltpu.roll(ext, 1, 0), ext, pltpu.roll(ext, n_ext - 1, 0))
        acc = jnp.zeros((TILE, c1 - c0), F32)
        for kh in range(3):
            for kw in range(3):
                tap = shifted[kw][kh * GRID_W:kh * GRID_W + TILE]
                ok = None
                for m in (row_ok[kh], col_ok[kw]):
                    if m is not None:
                        ok = m if ok is None else jnp.logical_and(ok, m)
                if ok is not None:
                    tap = jnp.where(ok, tap, 0.0)
                acc = acc + tap * w_ref[kh * 3 + kw:kh * 3 + kw + 1, c0:c1]
        return acc * _sigmoid(acc)

    nq = ML_HEADS * ML_QK
    q_ref[...] = conv(qk_m, qk_p, qk_n, wq_ref, 0, nq) * (ML_QK ** -0.5)
    k_ref[...] = conv(qk_m, qk_p, qk_n, wq_ref, nq, 2 * nq)
    nh = HG_HEADS * HG_DK
    hq_ref[...] = conv(qi_m, qi_p, qi_n, wi_ref, 0, nh)
    hv_ref[...] = conv(qi_m, qi_p, qi_n, wi_ref, nh, 2 * nh)


def _conv_call(qk, qi, wq, wi, tpb):
    n = qk.shape[0]
    rpt = TILE // GRID_W
    nrow = n // GRID_W
    cq, ci = qk.shape[1], qi.shape[1]

    def main(i):
        return (i, 0)

    def prev(i):
        return (jnp.maximum(i * rpt - 1, 0), 0)

    def nxt(i):
        return (jnp.minimum((i + 1) * rpt, nrow - 1), 0)

    return pl.pallas_call(
        functools.partial(_conv_kernel, tpb=tpb),
        grid=(n // TILE,),
        in_specs=[
            pl.BlockSpec((TILE, cq), main), pl.BlockSpec((GRID_W, cq), prev), pl.BlockSpec((GRID_W, cq), nxt),
            pl.BlockSpec((TILE, ci), main), pl.BlockSpec((GRID_W, ci), prev), pl.BlockSpec((GRID_W, ci), nxt),
            pl.BlockSpec((9, cq), lambda i: (0, 0)),
            pl.BlockSpec((9, ci), lambda i: (0, 0)),
        ],
        out_specs=[pl.BlockSpec((TILE, s), main) for s in (cq // 2, cq // 2, ci // 2, ci // 2)],
        out_shape=[jax.ShapeDtypeStruct((n, s), F32) for s in (cq // 2, cq // 2, ci // 2, ci // 2)],
        compiler_params=_cparams(1),
        name="conv",
    )(qk, qk, qk, qi, qi, qi, wq.reshape(9, cq), wi.reshape(9, ci))


W_CHUNKS = TILE // CHUNK


def _tile_maps(nct, nt):
    def fwd(b, j):
        return (b * nt + j, 0)

    def bwd(b, j):
        return (b * nt + jnp.where(j < nct, nct - 1 - j, nt + nct - 1 - j), 0)

    return fwd, bwd


def _tri(fwd):
    t = lax.broadcasted_iota(jnp.int32, (CHUNK, CHUNK), 0)
    s = lax.broadcasted_iota(jnp.int32, (CHUNK, CHUNK), 1)
    return (s <= t) if fwd else (s >= t)


def _interleave(gens):
    live = list(gens)
    while live:
        for g in list(live):
            try:
                next(g)
            except StopIteration:
                live.remove(g)


def _mask_dot(mask_bf, x):
    a = x.astype(BF16)
    r = x - a.astype(F32)
    b = r.astype(BF16)
    c = (r - b.astype(F32)).astype(BF16)
    return (_dot(mask_bf, a) + _dot(mask_bf, b)) + _dot(mask_bf, c)


def _mlstm_kernel(qf, kf, vf, gf, qb, kb, vb, gb, bias_ref, hf_ref, hb_ref, ct_ref, m_ref):
    @pl.when(pl.program_id(1) == 0)
    def _():
        ct_ref[...] = jnp.zeros_like(ct_ref)
        m_ref[...] = jnp.zeros_like(m_ref)

    assert CHUNK == ML_QK and 2 * ML_QK == LANES
    lane = lax.broadcasted_iota(jnp.int32, (CHUNK, LANES), 1)
    t_id = lax.broadcasted_iota(jnp.int32, (CHUNK, LANES), 0)
    lo = lane < ML_QK
    lo_row = lo[0:1, :]
    s_id = lane % ML_QK
    r128 = lax.broadcasted_iota(jnp.int32, (LANES, LANES), 0)
    c128 = lax.broadcasted_iota(jnp.int32, (LANES, LANES), 1)
    same_head = (r128 // ML_QK) == (c128 // ML_QK)
    ones_blk = jnp.ones((CHUNK, LANES), BF16)
    zeros_va = jnp.zeros((CHUNK, ML_V + LANES), BF16)

    def rep(arr, col):
        return jnp.broadcast_to(arr[:, col:col + 1], (CHUNK, LANES))

    def half_max(x, first):
        sel = lo if x.shape[0] == CHUNK else lo_row
        y = jnp.where(sel if first else jnp.logical_not(sel), x, -jnp.inf)
        return jnp.broadcast_to(jnp.max(y, axis=-1, keepdims=True), x.shape)

    dirs = ((qf, kf, vf, gf, hf_ref), (qb, kb, vb, gb, hb_ref))
    masks = [(jnp.where(_tri(fwd), 1.0, 0.0).astype(BF16), (s_id <= t_id) if fwd else (s_id >= t_id))
             for fwd in (True, False)]
    last_row = (CHUNK - 1, 0)
    orders = (tuple(range(W_CHUNKS)), tuple(range(W_CHUNKS - 1, -1, -1)))
    wv = ML_V + LANES
    shared, local = {}, {}

    def gates_of_chunk(d, c):
        rows = slice(c * CHUNK, (c + 1) * CHUNK)
        g_all = dirs[d][3][rows, :] + bias_ref[...]
        yield
        ls = _log_sigmoid(g_all)
        yield
        bcum = _mask_dot(masks[d][0], ls)
        yield
        g_t = g_all.T
        yield
        shared[(d, c)] = (g_all, bcum, g_t, bcum.T)
        yield

    def chunk_local(d, c, p):
        q_ref, k_ref, v_ref = dirs[d][:3]
        rows = slice(c * CHUNK, (c + 1) * CHUNK)
        g_all, bcum, g_t, b_t = shared[(d, c)]
        mask_p, last = masks[d][1], last_row[d]
        ci = [d * ML_HEADS + 2 * p + e for e in (0, 1)]
        cf = [(2 + d) * ML_HEADS + 2 * p + e for e in (0, 1)]
        b_rep = [rep(bcum, cf[e]) for e in (0, 1)]
        yield
        bp = jnp.where(lo, b_rep[0], b_rep[1])
        ip = jnp.where(lo, rep(g_all, ci[0]), rep(g_all, ci[1]))
        yield
        br = jnp.concatenate([b_t[cf[0]:cf[0] + 1, :], b_t[cf[1]:cf[1] + 1, :]], axis=1)
        ir = jnp.concatenate([g_t[ci[0]:ci[0] + 1, :], g_t[ci[1]:ci[1] + 1, :]], axis=1)
        yield
        dmat = jnp.where(mask_p, bp - br + ir, -jnp.inf)
        yield
        m_rep = [half_max(dmat, True), half_max(dmat, False)]
        yield
        q01 = q_ref[rows, p * LANES:(p + 1) * LANES].astype(BF16)
        k01 = k_ref[rows, p * LANES:(p + 1) * LANES]
        k01b = k01.astype(BF16)
        k_bd = jnp.where(same_head, jnp.concatenate([k01b, k01b], axis=0), jnp.zeros((), BF16))
        yield
        qk = _dot_nt(q01, k_bd)
        yield
        s = qk * jnp.exp(dmat - jnp.where(lo, m_rep[0], m_rep[1]))
        yield
        va = [jnp.concatenate([v_ref[rows, (2 * p + e) * ML_V:(2 * p + e + 1) * ML_V].astype(BF16), ones_blk], axis=1)
              for e in (0, 1)]
        v_bd = jnp.concatenate([jnp.concatenate([va[0], zeros_va], axis=1),
                                jnp.concatenate([zeros_va, va[1]], axis=1)], axis=0)
        yield
        nd_loc = _dot(s.astype(BF16), v_bd)
        yield
        gp = bp[last:last + 1, :]
        e_row = gp - br + ir
        me_rep = [half_max(e_row, True), half_max(e_row, False)]
        yield
        kw = (k01 * jnp.exp(gp - bp + ip - jnp.where(lo_row, me_rep[0], me_rep[1]))).astype(BF16)
        yield
        upd = _dot_tn(kw, jnp.concatenate(va, axis=1))
        yield
        g_tot = [b_rep[e][last:last + 1, :] for e in (0, 1)]
        local[(d, c, p)] = (q01, nd_loc, upd, b_rep, m_rep, g_tot, me_rep)
        yield

    def recurrence(d, p, e):
        h_ref = dirs[d][4]
        h = 2 * p + e
        ci = d * ML_HEADS + h
        m = m_ref[ci:ci + 1, :]
        ct = ct_ref[ci]
        zeros_ct = jnp.zeros((ML_QK, wv), BF16)
        yield
        for c in orders[d]:
            q01, nd_loc, upd, b_rep, m_rep, g_tot, me_rep = local[(d, c, p)]
            ct_b = ct.astype(BF16)
            qc = _dot(q01, jnp.concatenate([ct_b, zeros_ct] if e == 0 else [zeros_ct, ct_b], axis=0))
            yield
            a_inter = b_rep[e] + m
            m_t = jnp.maximum(a_inter, m_rep[e])
            yield
            w_loc = jnp.exp(m_rep[e] - m_t)
            w_int = jnp.exp(a_inter - m_t)
            yield
            num = w_loc * nd_loc[:, e * wv:e * wv + ML_V] + w_int * qc[:, 0:ML_V]
            den = w_loc * nd_loc[:, e * wv + ML_V:(e + 1) * wv] + w_int * qc[:, ML_V:wv]
            yield
            h_ref[c * CHUNK:(c + 1) * CHUNK, h * ML_V:(h + 1) * ML_V] = num / jnp.maximum(jnp.abs(den), jnp.exp(-m_t))
            yield
            m_new = jnp.maximum(g_tot[e] + m, me_rep[e])
            w_c = jnp.exp(g_tot[e] + m - m_new)
            w_u = jnp.exp(me_rep[e] - m_new)
            yield
            ct = (jnp.concatenate([w_c, w_c], axis=1) * ct
                  + jnp.concatenate([w_u, w_u], axis=1) * upd[e * ML_QK:(e + 1) * ML_QK, e * wv:(e + 1) * wv])
            m = m_new
            yield
        ct_ref[ci] = ct
        m_ref[ci:ci + 1, :] = m
        yield

    pairs = range(ML_HEADS // 2)
    _interleave([gates_of_chunk(d, c) for d in (0, 1) for c in orders[d]])
    _interleave([chunk_local(d, c, p) for c in range(W_CHUNKS) for d in (0, 1) for p in pairs])
    _interleave([recurrence(d, p, e) for d in (0, 1) for p in pairs for e in (0, 1)])


def _mlstm_call(q, k, v, gates, gate_b, n_batch, nct):
    n = q.shape[0]
    nt = n // TILE // n_batch
    fwd, bwd = _tile_maps(nct, nt)
    dq, dv = q.shape[1], v.shape[1]
    specs = lambda m: [pl.BlockSpec((TILE, dq), m), pl.BlockSpec((TILE, dq), m),
                       pl.BlockSpec((TILE, dv), m), pl.BlockSpec((TILE, LANES), m)]
    bias = jnp.zeros((1, LANES), F32).at[0, :4 * ML_HEADS].set(gate_b.reshape(-1))
    return pl.pallas_call(
        _mlstm_kernel,
        grid=(n_batch, nt),
        in_specs=specs(fwd) + specs(bwd) + [pl.BlockSpec((1, LANES), lambda b, j: (0, 0))],
        out_specs=[pl.BlockSpec((TILE, dv), fwd), pl.BlockSpec((TILE, dv), bwd)],
        out_shape=[jax.ShapeDtypeStruct((n, dv), F32)] * 2,
        scratch_shapes=[pltpu.VMEM((2 * ML_HEADS, ML_QK, ML_V + LANES), F32), pltpu.VMEM((2 * ML_HEADS, LANES), F32)],
        compiler_params=_cparams(2),
        name="mlstm",
    )(q, k, v, gates, q, k, v, gates, bias)


def _hgrn_kernel(qf, vf, ff, qb, vb, fb, fbias_ref, lb_ref, of_ref, ob_ref, st_ref):
    @pl.when(pl.program_id(1) == 0)
    def _():
        st_ref[...] = jnp.zeros_like(st_ref)

    n_sub = CHUNK // SUB
    rows = lax.broadcasted_iota(jnp.int32, (CHUNK, 1), 0)
    t_i = lax.broadcasted_iota(jnp.int32, (CHUNK, CHUNK), 0)
    s_i = lax.broadcasted_iota(jnp.int32, (CHUNK, CHUNK), 1)
    same_sub = (t_i // SUB) == (s_i // SUB)
    lb = lb_ref[...]
    dirs = ((qf, vf, ff, of_ref), (qb, vb, fb, ob_ref))
    masks = [_tri(fwd) for fwd in (True, False)]
    masks_bf = [jnp.where(m, 1.0, 0.0).astype(BF16) for m in masks]
    diag_ok = [jnp.logical_and(same_sub, m) for m in masks]
    orders = (tuple(range(W_CHUNKS)), tuple(range(W_CHUNKS - 1, -1, -1)))
    local = {}

    def chunk_local(d, c):
        q_ref, v_ref, f_ref, _ = dirs[d]
        fwd = d == 0
        last = CHUNK - 1 if fwd else 0
        rs = slice(c * CHUNK, (c + 1) * CHUNK)
        f = lb + (1.0 - lb) * _sigmoid(f_ref[rs, :] + fbias_ref[d:d + 1, :])
        yield
        kk = 1.0 - f
        lf = jnp.log(f)
        yield
        b = _mask_dot(masks_bf[d], lf)
        yield
        q = q_ref[rs, :]
        q_parts, k_parts = [], []
        for p in range(1, n_sub):
            if fwd:
                ref_row, q_ok, k_ok = p * SUB - 1, (rows // SUB) == p, rows < p * SUB
            else:
                ref_row, q_ok, k_ok = p * SUB, (rows // SUB) == p - 1, rows >= p * SUB
            r_p = b[ref_row:ref_row + 1, :]
            q_parts.append(jnp.where(q_ok, q * jnp.exp(jnp.minimum(b - r_p, 0.0)), 0.0))
            yield
            k_parts.append(jnp.where(k_ok, kk * jnp.exp(jnp.minimum(r_p - b, 0.0)), 0.0))
            yield
        mid = SUB // 2 - 1 if fwd else SUB // 2
        c_m = b[mid:mid + 1, :]
        for i in range(1, n_sub):
            c_m = jnp.where(rows >= i * SUB, b[i * SUB + mid:i * SUB + mid + 1, :], c_m)
        yield
        q_d = q * jnp.exp(b - c_m)
        yield
        k_d = kk * jnp.exp(c_m - b)
        yield
        g_row = b[last:last + 1, :]
        q_in = (q * jnp.exp(b)).astype(BF16)
        yield
        k_out = (kk * jnp.exp(g_row - b)).astype(BF16)
        decay = jnp.exp(g_row)
        yield
        for h in range(HG_HEADS):
            sl = slice(h * HG_DK, (h + 1) * HG_DK)
            qc = jnp.concatenate([x[:, sl] for x in q_parts], axis=1).astype(BF16)
            kc = jnp.concatenate([x[:, sl] for x in k_parts], axis=1).astype(BF16)
            yield
            a_off = _dot_nt(qc, kc)
            yield
            a_diag = _dot_nt(q_d[:, sl].astype(BF16), k_d[:, sl].astype(BF16))
            yield
            a = (a_off + jnp.where(diag_ok[d], a_diag, 0.0)).astype(BF16)
            vh = v_ref[rs, sl].astype(BF16)
            yield
            o_loc = _dot(a, vh)
            yield
            local[(d, c, h)] = (o_loc, q_in[:, sl], decay[:, sl], _dot_tn(vh, k_out[:, sl]))
            yield

    def recurrence(d, h):
        o_ref = dirs[d][3]
        st = st_ref[d * HG_HEADS + h]
        yield
        for c in orders[d]:
            o_loc, q_in, decay, upd = local[(d, c, h)]
            inter = _dot_nt(q_in, st.astype(BF16))
            yield
            o_ref[c * CHUNK:(c + 1) * CHUNK, h * HG_DK:(h + 1) * HG_DK] = o_loc + inter
            st = st * decay + upd
            yield
        st_ref[d * HG_HEADS + h] = st
        yield

    _interleave([chunk_local(d, c) for c in range(W_CHUNKS) for d in (0, 1)])
    _interleave([recurrence(d, h) for d in (0, 1) for h in range(HG_HEADS)])


def _hgrn_call(hq, hv, ff, fb, f_b, lb, n_batch, nct):
    n, w = hq.shape
    nt = n // TILE // n_batch
    fwd, bwd = _tile_maps(nct, nt)
    spec = lambda m: pl.BlockSpec((TILE, w), m)
    return pl.pallas_call(
        _hgrn_kernel,
        grid=(n_batch, nt),
        in_specs=[spec(fwd)] * 3 + [spec(bwd)] * 3 + [pl.BlockSpec((2, w), lambda b, j: (0, 0)),
                                                      pl.BlockSpec((1, w), lambda b, j: (0, 0))],
        out_specs=[spec(fwd), spec(bwd)],
        out_shape=[jax.ShapeDtypeStruct((n, w), F32)] * 2,
        scratch_shapes=[pltpu.VMEM((2 * HG_HEADS, HG_DK, HG_DK), F32)],
        compiler_params=_cparams(2),
        name="hgrn",
    )(hq, hv, ff, hq, hv, fb, f_b, lb.reshape(1, w))


def _head_rms(h, g, width):
    parts = []
    for i in range(h.shape[1] // width):
        p = h[:, i * width:(i + 1) * width]
        parts.append(p * lax.rsqrt(jnp.mean(p * p, axis=-1, keepdims=True) + EPS))
    return jnp.concatenate(parts, axis=1) * g


def _out_kernel(x_ref, mlf, mlb, hgf, hgb, o_ref, go_ref, mod_ref, gml_ref, ghg_ref, wout_ref, g2_ref, rw_ref, rb_ref,
                xo_ref, h2_ref, idx_ref, gate_ref, *, tpb, n_batch):
    d = x_ref.shape[1]

    def row_group(r0):
        rs = slice(r0, r0 + OUT_ROWS)
        mod = _mod_row(mod_ref, tpb, n_batch, pl.program_id(0) * OUT_TILES + r0 // TILE)
        y_ml = _sigmoid(o_ref[rs, :]) * _head_rms(mlf[rs, :] + mlb[rs, :], gml_ref[...], ML_V)
        yield
        go = go_ref[rs, :]
        y_hg = go * _sigmoid(go) * _head_rms(hgf[rs, :] + hgb[rs, :], ghg_ref[...], HG_DK)
        yield
        y = jnp.concatenate([y_ml, y_hg], axis=1).astype(BF16)
        proj = _dot(y, wout_ref[...])
        yield
        x = x_ref[rs, :] + mod[:, 2 * d:3 * d] * proj
        xo_ref[rs, :] = x
        yield
        h2 = _rms(x, g2_ref[...]) * (1.0 + mod[:, 4 * d:5 * d]) + mod[:, 3 * d:4 * d]
        yield
        _store_slabs(h2_ref, h2, r0 * SLAB)
        yield
        h_hi = h2.astype(BF16)
        r1 = h2 - h_hi.astype(F32)
        h_mid = r1.astype(BF16)
        h_lo = (r1 - h_mid.astype(F32)).astype(BF16)
        yield
        p_hi = _dot(h_hi, rw_ref[...])
        yield
        p_mid = _dot(h_mid, rw_ref[:, :2 * LANES])
        yield
        p_lo = _dot(h_lo, rw_ref[:, :LANES])
        yield
        small = (p_lo + p_mid[:, LANES:]) + p_hi[:, 2 * LANES:]
        vals = ((small + (p_mid[:, :LANES] + p_hi[:, LANES:2 * LANES])) + p_hi[:, :LANES]) + rb_ref[...]
        yield
        lane = lax.broadcasted_iota(jnp.int32, vals.shape, 1)
        lane_f = lane.astype(F32)
        idx_out = jnp.zeros(vals.shape, F32)
        top = []
        for k in range(TOP_K):
            mx = jnp.max(vals, axis=-1, keepdims=True)
            yield
            ix = jnp.min(jnp.where(vals == mx, lane_f, float(LANES)), axis=-1, keepdims=True)
            yield
            top.append(mx)
            idx_out = jnp.where(lane == k, ix, idx_out)
            vals = jnp.where(lane_f == ix, -jnp.inf, vals)
            yield
        ex = [jnp.exp(t - top[0]) for t in top]
        tot = ex[0] + ex[1] + ex[2] + ex[3]
        yield
        gate_out = jnp.zeros(vals.shape, F32)
        for k in range(TOP_K):
            gate_out = jnp.where(lane == k, ex[k] / tot, gate_out)
        idx_ref[rs, :] = idx_out.astype(jnp.int32)
        gate_ref[rs, :] = gate_out
        yield

    _interleave([row_group(r0) for r0 in range(0, OUT_TILES * TILE, OUT_ROWS)])


def _out_call(x, mlf, mlb, hgf, hgb, o, go, mod_l, gml, ghg, wout, g2, rw, rb, tpb, n_batch):
    n, d = x.shape
    w = mlf.shape[1]
    row = lambda i: (i, 0)
    fix = lambda i: (0, 0)
    rw_p = jnp.zeros((d, LANES), F32).at[:, :N_EXPERTS].set(rw)
    rw_hi = rw_p.astype(BF16)
    rw_r1 = rw_p - rw_hi.astype(F32)
    rw_mid = rw_r1.astype(BF16)
    rw_cat = jnp.concatenate([rw_hi, rw_mid, (rw_r1 - rw_mid.astype(F32)).astype(BF16)], axis=1)
    rb_p = jnp.full((1, LANES), NEG, F32).at[0, :N_EXPERTS].set(rb)
    rows = OUT_TILES * TILE
    assert n % rows == 0
    return pl.pallas_call(
        functools.partial(_out_kernel, tpb=tpb, n_batch=n_batch),
        grid=(n // rows,),
        in_specs=[pl.BlockSpec((rows, d), row)] + [pl.BlockSpec((rows, w), row)] * 6 + [
            pl.BlockSpec((8, 6 * d), fix), pl.BlockSpec((1, w), fix), pl.BlockSpec((1, w), fix),
            pl.BlockSpec((d, d), fix), pl.BlockSpec((1, d), fix), pl.BlockSpec((d, 3 * LANES), fix),
            pl.BlockSpec((1, LANES), fix)],
        out_specs=[pl.BlockSpec((rows, d), row), pl.BlockSpec((rows * SLAB, LANES), row),
                   pl.BlockSpec((rows, LANES), row), pl.BlockSpec((rows, LANES), row)],
        out_shape=[jax.ShapeDtypeStruct((n, d), F32), jax.ShapeDtypeStruct((n * SLAB, LANES), F32),
                   jax.ShapeDtypeStruct((n, LANES), jnp.int32), jax.ShapeDtypeStruct((n, LANES), F32)],
        compiler_params=_cparams(1),
        name="outproj",
    )(x, mlf, mlb, hgf, hgb, o, go, mod_l, gml.reshape(1, w), ghg.reshape(1, w), wout, g2.reshape(1, d), rw_cat, rb_p)


def _onehots(idx):
    lane = lax.broadcasted_iota(jnp.int32, idx.shape, 1)
    return [lane == idx[:, k:k + 1] for k in range(TOP_K)]


def _rank_kernel(idx_ref, rank_ref, cnt_ref, carry_ref):
    @pl.when(pl.program_id(0) == 0)
    def _():
        carry_ref[...] = jnp.zeros_like(carry_ref)

    hots = _onehots(idx_ref[...])
    m = jnp.zeros(idx_ref.shape, F32)
    for hk in hots:
        m = m + jnp.where(hk, 1.0, 0.0)
    t_i = lax.broadcasted_iota(jnp.int32, (TILE, TILE), 0)
    s_i = lax.broadcasted_iota(jnp.int32, (TILE, TILE), 1)
    before = _dot((s_i < t_i).astype(BF16), m.astype(BF16)) + carry_ref[...]
    lane = lax.broadcasted_iota(jnp.int32, idx_ref.shape, 1)
    out = jnp.zeros(idx_ref.shape, F32)
    for k, hk in enumerate(hots):
        rk = jnp.sum(jnp.where(hk, before, 0.0), axis=-1, keepdims=True)
        out = jnp.where(lane == k, rk, out)
    rank_ref[...] = out.astype(jnp.int32)
    carry_ref[...] = carry_ref[...] + jnp.sum(m, axis=0, keepdims=True)
    cnt_ref[...] = carry_ref[...]


def _rank_call(idx):
    n = idx.shape[0]
    return pl.pallas_call(
        _rank_kernel,
        grid=(n // TILE,),
        in_specs=[pl.BlockSpec((TILE, LANES), lambda i: (i, 0))],
        out_specs=[pl.BlockSpec((TILE, LANES), lambda i: (i, 0)), pl.BlockSpec((1, LANES), lambda i: (0, 0))],
        out_shape=[jax.ShapeDtypeStruct((n, LANES), jnp.int32), jax.ShapeDtypeStruct((1, LANES), F32)],
        scratch_shapes=[pltpu.VMEM((1, LANES), F32)],
        compiler_params=_cparams(1),
        name="rank",
    )(idx)


def _dest_kernel(idx_ref, rank_ref, cnt_ref, dest_ref, meta_ref, *, n_blocks):
    cnt = cnt_ref[...]
    padded = jnp.floor((cnt + (MOE_BLK - 1)) * (1.0 / MOE_BLK)) * MOE_BLK
    e_i = lax.broadcasted_iota(jnp.int32, (LANES, LANES), 0)
    e_j = lax.broadcasted_iota(jnp.int32, (LANES, LANES), 1)
    pad_start = _dot_hi(jnp.broadcast_to(padded, (8, LANES)), (e_i < e_j).astype(F32))[0:1]
    pad_end = pad_start + padded
    lane = lax.broadcasted_iota(jnp.int32, idx_ref.shape, 1)
    rank = rank_ref[...]
    out = jnp.zeros(idx_ref.shape, jnp.int32)
    for k, hk in enumerate(_onehots(idx_ref[...])):
        st = jnp.sum(jnp.where(hk, pad_start, 0.0), axis=-1, keepdims=True).astype(jnp.int32)
        out = jnp.where(lane == k, st + rank[:, k:k + 1], out)
    dest_ref[...] = out
    blk = lax.broadcasted_iota(jnp.int32, meta_ref.shape, 0).astype(F32) * MOE_BLK
    lane_m = lax.broadcasted_iota(jnp.int32, meta_ref.shape, 1)
    done = jnp.where(jnp.logical_and(lane_m < N_EXPERTS, pad_end <= blk), 1.0, 0.0)
    be = jnp.minimum(jnp.sum(done, axis=-1, keepdims=True), N_EXPERTS - 1.0)
    used = jnp.sum(jnp.where(lane_m[0:1] < N_EXPERTS, padded, 0.0), axis=-1, keepdims=True) * (1.0 / MOE_BLK)
    diag = lane_m == lax.broadcasted_iota(jnp.int32, meta_ref.shape, 0)
    end_col = jnp.sum(jnp.where(diag, pad_end, 0.0), axis=-1, keepdims=True)
    pad_col = jnp.sum(jnp.where(diag, padded, 0.0), axis=-1, keepdims=True)
    meta = jnp.where(lane_m == 0, be, jnp.where(lane_m == 1, used, jnp.where(lane_m == 2, end_col, pad_col)))
    meta_ref[...] = jnp.where(lane_m < 4, meta, 0.0).astype(jnp.int32)


def _dest_call(idx, rank, cnt, n_blocks):
    n = idx.shape[0]
    nb_pad = -(-n_blocks // 8) * 8
    row = lambda i: (i, 0)
    return pl.pallas_call(
        functools.partial(_dest_kernel, n_blocks=n_blocks),
        grid=(n // TILE,),
        in_specs=[pl.BlockSpec((TILE, LANES), row), pl.BlockSpec((TILE, LANES), row),
                  pl.BlockSpec((1, LANES), lambda i: (0, 0))],
        out_specs=[pl.BlockSpec((TILE, LANES), row), pl.BlockSpec((nb_pad, LANES), lambda i: (0, 0))],
        out_shape=[jax.ShapeDtypeStruct((n, LANES), jnp.int32), jax.ShapeDtypeStruct((nb_pad, LANES), jnp.int32)],
        compiler_params=_cparams(1),
        name="dest",
    )(idx, rank, cnt)


ROW_UNROLL = 8


def _scatter_kernel(dest_ref, zinfo_ref, h_ref, xb_ref, zbuf, sem):
    blk_rows = MOE_BLK * SLAB

    @pl.when(pl.program_id(0) == 0)
    def _():
        zbuf[...] = jnp.zeros_like(zbuf)
        n_blocks = xb_ref.shape[0] // blk_rows
        used = zinfo_ref[2 * N_EXPERTS]
        for stage in ("start", "wait"):
            for e in range(N_EXPERTS):
                @pl.when(zinfo_ref[2 * e + 1] > 0)
                def _():
                    first = pl.multiple_of((zinfo_ref[2 * e] - MOE_BLK) * SLAB, blk_rows)
                    cp = pltpu.make_async_copy(zbuf, xb_ref.at[pl.ds(first, blk_rows), :], sem)
                    cp.start() if stage == "start" else cp.wait()

                @pl.when(used + e < n_blocks)
                def _():
                    first = pl.multiple_of((used + e) * blk_rows, blk_rows)
                    cp = pltpu.make_async_copy(zbuf, xb_ref.at[pl.ds(first, blk_rows), :], sem)
                    cp.start() if stage == "start" else cp.wait()

    def issue(g, c):
        for u in range(ROW_UNROLL):
            src = h_ref.at[pl.ds(pl.multiple_of((g * ROW_UNROLL + u) * SLAB, SLAB), SLAB), :]
            for k in range(TOP_K):
                dst = pl.multiple_of(dest_ref[g * (ROW_UNROLL * TOP_K) + u * TOP_K + k] * SLAB, SLAB)
                pltpu.make_async_copy(src, xb_ref.at[pl.ds(dst, SLAB), :], sem).start(priority=k % 2)
        return c

    lax.fori_loop(0, TILE // ROW_UNROLL, issue, 0)
    all_rows = xb_ref.at[pl.ds(0, TILE * TOP_K * SLAB), :]
    pltpu.make_async_copy(all_rows, all_rows, sem).wait()


def _scatter_call(dest_flat, zinfo, h2s, n_rows):
    n = h2s.shape[0] // SLAB
    return pl.pallas_call(
        _scatter_kernel,
        grid=(n // TILE,),
        in_specs=[pl.BlockSpec((TILE * TOP_K,), lambda i: (i,), memory_space=pltpu.SMEM),
                  pl.BlockSpec(memory_space=pltpu.SMEM),
                  pl.BlockSpec((TILE * SLAB, LANES), lambda i: (i, 0))],
        out_specs=pl.BlockSpec(memory_space=pl.ANY),
        out_shape=jax.ShapeDtypeStruct((n_rows * SLAB, LANES), F32),
        scratch_shapes=[pltpu.VMEM((MOE_BLK * SLAB, LANES), F32), pltpu.SemaphoreType.DMA(())],
        compiler_params=_cparams(1),
        name="scatter",
    )(dest_flat, zinfo, h2s)


CAST_ROWS = 64


def _cast_rows(src_ref, dst_ref):
    def body(r, c):
        rows = pl.ds(pl.multiple_of(r * CAST_ROWS, CAST_ROWS), CAST_ROWS)
        dst_ref[rows, :] = src_ref[rows, :].astype(BF16)
        return c

    lax.fori_loop(0, src_ref.shape[0] // CAST_ROWS, body, 0)


def _expert_kernel(be_ref, used_ref, nblk_ref, x_ref, wgu_hbm, bgu_ref, wd_hbm, bd_ref, y_ref,
                   wgu_f, wd_f, wgu_bf, wd_bf, slot_ref, sem, *, layer):
    i = pl.program_id(0)
    used = used_ref[0]
    live = i < used
    e = be_ref[i]
    new_expert = jnp.logical_or(i == 0, e != be_ref[jnp.maximum(i - 1, 0)])

    def fetch(expert, slot):
        return (pltpu.make_async_copy(wgu_hbm.at[layer, expert], wgu_f.at[slot], sem.at[0, slot]),
                pltpu.make_async_copy(wd_hbm.at[layer, expert], wd_f.at[slot], sem.at[1, slot]))

    @pl.when(jnp.logical_and(live, i == 0))
    def _():
        slot_ref[0] = 0
        for cp in fetch(e, 0):
            cp.start()

    @pl.when(jnp.logical_not(live))
    def _():
        y_ref[...] = jnp.zeros_like(y_ref)

    @pl.when(jnp.logical_and(live, new_expert))
    def _():
        slot = slot_ref[0]
        for cp in fetch(e, slot):
            cp.wait()
        nxt = i + nblk_ref[e]

        @pl.when(nxt < used)
        def _():
            for cp in fetch(be_ref[nxt], 1 - slot):
                cp.start()

        _cast_rows(wgu_f.at[slot], wgu_bf)
        _cast_rows(wd_f.at[slot], wd_bf)
        slot_ref[0] = 1 - slot

    @pl.when(live)
    def _():
        de = wd_bf.shape[0]
        gu = _dot(_load_slabs(x_ref, MOE_BLK).astype(BF16), wgu_bf[...]) + bgu_ref[...]
        glu = jnp.minimum(gu[:, :de], SWIGLU_LIMIT)
        lin = jnp.clip(gu[:, de:], -SWIGLU_LIMIT, SWIGLU_LIMIT)
        act = glu * _sigmoid(SWIGLU_ALPHA * glu) * (lin + 1.0)
        _store_slabs(y_ref, _dot(act.astype(BF16), wd_bf[...]) + bd_ref[...])


def _expert_call(layer, be, used, nblk, xb, wgu, bgu, wd, bd):
    n_rows = xb.shape[0] // SLAB
    depth, e, d, de2 = wgu.shape
    nb = n_rows // MOE_BLK
    last_live = lambda i, used: jnp.maximum(jnp.minimum(i, used[0] - 1), 0)
    blk = lambda i, be, used, nblk: (last_live(i, used), 0)
    bsel = lambda i, be, used, nblk: (layer, be[last_live(i, used)], 0, 0)
    return pl.pallas_call(
        functools.partial(_expert_kernel, layer=layer),
        grid_spec=pltpu.PrefetchScalarGridSpec(
            num_scalar_prefetch=3,
            grid=(nb,),
            in_specs=[pl.BlockSpec((MOE_BLK * SLAB, LANES), blk),
                      pl.BlockSpec(memory_space=pl.ANY), pl.BlockSpec((None, None, 1, de2), bsel),
                      pl.BlockSpec(memory_space=pl.ANY), pl.BlockSpec((None, None, 1, d), bsel)],
            out_specs=pl.BlockSpec((MOE_BLK * SLAB, LANES), lambda i, be, used, nblk: (i, 0)),
            scratch_shapes=[pltpu.VMEM((2, d, de2), F32), pltpu.VMEM((2, de2 // 2, d), F32),
                            pltpu.VMEM((d, de2), BF16), pltpu.VMEM((de2 // 2, d), BF16),
                            pltpu.SMEM((1,), jnp.int32), pltpu.SemaphoreType.DMA((2, 2))],
        ),
        out_shape=jax.ShapeDtypeStruct((n_rows * SLAB, LANES), F32),
        compiler_params=_cparams(1, vmem=EXPERT_VMEM_LIMIT),
        name="expert",
    )(be, used, nblk, xb, wgu, bgu.reshape(depth, e, 1, de2), wd, bd.reshape(depth, e, 1, d))


def _combine_kernel(dest_ref, x_ref, gate_ref, mod_ref, fg_ref, yb_ref, o_ref, buf, sem, *, tpb, n_batch, final):
    def issue(g, c):
        for u in range(ROW_UNROLL):
            rows = pl.ds(pl.multiple_of((g * ROW_UNROLL + u) * SLAB, SLAB), SLAB)
            for k in range(TOP_K):
                src = pl.multiple_of(dest_ref[g * (ROW_UNROLL * TOP_K) + u * TOP_K + k] * SLAB, SLAB)
                pltpu.make_async_copy(yb_ref.at[pl.ds(src, SLAB), :], buf.at[k, rows, :], sem).start(priority=k % 2)
        return c

    lax.fori_loop(0, TILE // ROW_UNROLL, issue, 0)
    pltpu.make_async_copy(buf, buf, sem).wait()
    d = x_ref.shape[1]
    mod = _mod_row(mod_ref, tpb, n_batch)
    gates = gate_ref[...]
    f = gates[:, 0:1] * _load_slabs(buf.at[0], TILE)
    for k in range(1, TOP_K):
        f = f + gates[:, k:k + 1] * _load_slabs(buf.at[k], TILE)
    x = x_ref[...] + mod[:, 5 * d:6 * d] * f
    o_ref[...] = _rms(x, fg_ref[...]) if final else x


def _combine_call(dest_flat, x, gates, mod_l, fg, yb, tpb, n_batch, final):
    n, d = x.shape
    if final:
        out_rows = n - n_batch * TILE
        omap = lambda i: ((i // tpb) * (tpb - 1) + jnp.maximum(i % tpb - 1, 0), 0)
    else:
        out_rows = n
        omap = lambda i: (i, 0)
    return pl.pallas_call(
        functools.partial(_combine_kernel, tpb=tpb, n_batch=n_batch, final=final),
        grid=(n // TILE,),
        in_specs=[pl.BlockSpec((TILE * TOP_K,), lambda i: (i,), memory_space=pltpu.SMEM),
                  pl.BlockSpec((TILE, d), lambda i: (i, 0)),
                  pl.BlockSpec((TILE, LANES), lambda i: (i, 0)),
                  pl.BlockSpec((8, 6 * d), lambda i: (0, 0)),
                  pl.BlockSpec((1, d), lambda i: (0, 0)),
                  pl.BlockSpec(memory_space=pl.ANY)],
        out_specs=pl.BlockSpec((TILE, d), omap),
        out_shape=jax.ShapeDtypeStruct((out_rows, d), F32),
        scratch_shapes=[pltpu.VMEM((TOP_K, TILE * SLAB, LANES), F32), pltpu.SemaphoreType.DMA(())],
        compiler_params=_cparams(1),
        name="combine",
    )(dest_flat, x, gates, mod_l, fg.reshape(1, d), yb)


def kernel(x, c, ctx, c_ctx, w_ada, b_ada, norm1_g, w_in, mlstm_conv, mlstm_gate_b, mlstm_norm_g, hgrn_conv, hgrn_f_b,
           hgrn_lb_raw, hgrn_norm_g, w_out, norm2_g, router_w, router_b, w_gu, b_gu, w_down, b_down, final_g):
    n_batch, seq, d = x.shape
    ctx_len = ctx.shape[1]
    depth = w_ada.shape[0]
    assert ctx_len == TILE and seq % TILE == 0 and n_batch + 1 <= 8 and d == SLAB * LANES
    tpb = (ctx_len + seq) // TILE
    nct = ctx_len // TILE
    n = n_batch * (ctx_len + seq)
    n_blocks = -(-(n * TOP_K) // MOE_BLK) + N_EXPERTS
    n_rows = n_blocks * MOE_BLK

    xa = jnp.concatenate([ctx, x], axis=1).reshape(n, d)
    cond = jnp.zeros((8, d), F32).at[:n_batch].set(c).at[n_batch].set(c_ctx)
    mod = _ada_call(cond, w_ada, b_ada)

    lb_w = jax.nn.softmax(hgrn_lb_raw.astype(F32), axis=0)
    lower = jnp.cumsum(lb_w, axis=0) - lb_w[0]

    n_gate = 4 * ML_HEADS
    g0 = 3 * 512
    w_in_p = jnp.concatenate([w_in[:, :, :g0], jnp.pad(w_in[:, :, g0:g0 + n_gate], ((0, 0), (0, 0), (0, LANES - n_gate))),
                              w_in[:, :, g0 + n_gate:]], axis=2).astype(BF16)
    w_out_b = w_out.astype(BF16)

    for l in range(depth):
        last = l == depth - 1
        qk, v, o, gates, qi, ff, fb, go = _inproj_call(xa, mod[l], norm1_g[l], w_in_p[l], tpb, n_batch)
        q, k, hq, hv = _conv_call(qk, qi, mlstm_conv[l], hgrn_conv[l], tpb)
        mlf, mlb = _mlstm_call(q, k, v, gates, mlstm_gate_b[l], n_batch, nct)
        hgf, hgb = _hgrn_call(hq, hv, ff, fb, hgrn_f_b[l], lower[l], n_batch, nct)
        xa, h2, idx, gate = _out_call(xa, mlf, mlb, hgf, hgb, o, go, mod[l], mlstm_norm_g[l], hgrn_norm_g[l], w_out_b[l],
                                      norm2_g[l], router_w[l], router_b[l], tpb, n_batch)
        rank, cnt = _rank_call(idx)
        dest, meta = _dest_call(idx, rank, cnt, n_blocks)
        dest_flat = dest[:, :TOP_K].reshape(-1)
        zinfo = jnp.concatenate([meta[:N_EXPERTS, 2:4].reshape(-1), meta[0:1, 1]])
        xb = _scatter_call(dest_flat, zinfo, h2, n_rows)
        nblk = meta[:N_EXPERTS, 3] // MOE_BLK
        yb = _expert_call(l, meta[:n_blocks, 0], meta[0:1, 1], nblk, xb, w_gu, b_gu, w_down, b_down)
        xa = _combine_call(dest_flat, xa, gate, mod[l], final_g, yb, tpb, n_batch, last)
    return xa.reshape(n_batch, seq, d)
```

```python
import functools

import jax
import jax.numpy as jnp
from jax import lax
from jax.experimental import pallas as pl
from jax.experimental.pallas import tpu as pltpu

F32 = jnp.float32
BF16 = jnp.bfloat16
HIGHEST = lax.Precision.HIGHEST

GRID_W = 64
CHUNK = 64
ML_HEADS = 4
ML_QK = 64
ML_V = 128
HG_HEADS = 4
HG_DK = 128
N_EXPERTS = 32
TOP_K = 4
SWIGLU_LIMIT = 7.0
SWIGLU_ALPHA = 1.702
EPS = 1e-6

TILE = 256
OUT_TILES = 2
OUT_ROWS = 128
MOE_BLK = 256
LANES = 128
SUB = 16
NEG = -1e30
VMEM_LIMIT = 48 * 1024 * 1024
EXPERT_VMEM_LIMIT = 56 * 1024 * 1024


def _cparams(n_axes, vmem=VMEM_LIMIT):
    return pltpu.CompilerParams(dimension_semantics=("arbitrary",) * n_axes, vmem_limit_bytes=vmem)


def _dot(a, b):
    return jnp.dot(a, b, preferred_element_type=F32)


def _dot_nt(a, b):
    return lax.dot_general(a, b, (((1,), (1,)), ((), ())), preferred_element_type=F32)


def _dot_tn(a, b):
    return lax.dot_general(a, b, (((0,), (0,)), ((), ())), preferred_element_type=F32)


def _dot_hi(a, b):
    return jnp.dot(a, b, precision=HIGHEST, preferred_element_type=F32)


def _rms(x, g):
    return x * lax.rsqrt(jnp.mean(x * x, axis=-1, keepdims=True) + EPS) * g


def _sigmoid(x):
    return 1.0 / (1.0 + jnp.exp(-x))


def _log_sigmoid(x):
    return jnp.minimum(x, 0.0) - jnp.log(1.0 + jnp.exp(-jnp.abs(x)))


SLAB = 8


def _store_slabs(ref, x, base=0):
    rows = x.shape[0]
    for s in range(SLAB):
        ref[pl.ds(base + s, rows, stride=SLAB), :] = x[:, s * LANES:(s + 1) * LANES]


def _load_slabs(ref, rows):
    return jnp.concatenate([ref[pl.ds(s, rows, stride=SLAB), :] for s in range(SLAB)], axis=1)


def _ada_kernel(cond_ref, w_ref, b_ref, o_ref):
    c = cond_ref[...]
    o_ref[...] = _dot_hi(c * _sigmoid(c), w_ref[...]) + b_ref[...]


def _ada_call(cond, w_ada, b_ada):
    depth, d, n6 = w_ada.shape
    tn = 1024
    return pl.pallas_call(
        _ada_kernel,
        grid=(depth, n6 // tn),
        in_specs=[
            pl.BlockSpec((8, d), lambda l, j: (0, 0)),
            pl.BlockSpec((None, d, tn), lambda l, j: (l, 0, j)),
            pl.BlockSpec((None, 1, tn), lambda l, j: (l, 0, j)),
        ],
        out_specs=pl.BlockSpec((None, 8, tn), lambda l, j: (l, 0, j)),
        out_shape=jax.ShapeDtypeStruct((depth, 8, n6), F32),
        compiler_params=_cparams(2),
        name="ada",
    )(cond, w_ada, b_ada.reshape(depth, 1, n6))


def _mod_row(mod_ref, tpb, n_batch, tile=None):
    i = pl.program_id(0) if tile is None else tile
    row = jnp.where(i % tpb == 0, n_batch, i // tpb)
    return mod_ref[pl.ds(row, 1), :]


IN_SEGS = (512, 512, 512, 128, 1024, 512, 512, 512)


def _inproj_kernel(x_ref, mod_ref, g_ref, w_ref, *outs, tpb, n_batch):
    d = x_ref.shape[1]
    mod = _mod_row(mod_ref, tpb, n_batch)
    h = _rms(x_ref[...], g_ref[...]) * (1.0 + mod[:, d:2 * d]) + mod[:, 0:d]
    hb = h.astype(BF16)
    off = 0
    for o_ref, n in zip(outs, IN_SEGS):
        o_ref[...] = _dot(hb, w_ref[:, off:off + n])
        off += n


def _inproj_call(x, mod_l, g, w_p, tpb, n_batch):
    n, d = x.shape
    nw = w_p.shape[1]
    return pl.pallas_call(
        functools.partial(_inproj_kernel, tpb=tpb, n_batch=n_batch),
        grid=(n // TILE,),
        in_specs=[
            pl.BlockSpec((TILE, d), lambda i: (i, 0)),
            pl.BlockSpec((8, 6 * d), lambda i: (0, 0)),
            pl.BlockSpec((1, d), lambda i: (0, 0)),
            pl.BlockSpec((d, nw), lambda i: (0, 0)),
        ],
        out_specs=[pl.BlockSpec((TILE, s), lambda i: (i, 0)) for s in IN_SEGS],
        out_shape=[jax.ShapeDtypeStruct((n, s), F32) for s in IN_SEGS],
        compiler_params=_cparams(1),
        name="inproj",
    )(x, mod_l, g.reshape(1, d), w_p)


def _conv_kernel(qk_m, qk_p, qk_n, qi_m, qi_p, qi_n, wq_ref, wi_ref, q_ref, k_ref, hq_ref, hv_ref, *, tpb):
    r = lax.broadcasted_iota(jnp.int32, (TILE, 1), 0)
    j = jnp.zeros((TILE, 1), jnp.int32) + pl.program_id(0) % tpb
    is_ctx = j == 0
    col = r % GRID_W
    ok_l = jnp.logical_or(col != 0, jnp.logical_and(is_ctx, r != 0))
    ok_r = jnp.logical_or(col != GRID_W - 1, jnp.logical_and(is_ctx, r != TILE - 1))
    ok_u = jnp.logical_and(jnp.logical_not(is_ctx), jnp.logical_or(j != 1, r >= GRID_W))
    ok_d = jnp.logical_and(jnp.logical_not(is_ctx), jnp.logical_or(j != tpb - 1, r < TILE - GRID_W))
    col_ok = (ok_l, None, ok_r)
    row_ok = (ok_u, None, ok_d)
    n_ext = TILE + 2 * GRID_W

    def conv(main, prev, nxt, w_ref, c0, c1):
        ext = jnp.concatenate([prev[:, c0:c1], main[:, c0:c1], nxt[:, c0:c1]], axis=0)
        shifted = (pltpu.roll(ext, 1, 0), ext, pltpu.roll(ext, n_ext - 1, 0))
        acc = jnp.zeros((TILE, c1 - c0), F32)
        for kh in range(3):
            for kw in range(3):
                tap = shifted[kw][kh * GRID_W:kh * GRID_W + TILE]
                ok = None
                for m in (row_ok[kh], col_ok[kw]):
                    if m is not None:
                        ok = m if ok is None else jnp.logical_and(ok, m)
                if ok is not None:
                    tap = jnp.where(ok, tap, 0.0)
                acc = acc + tap * w_ref[kh * 3 + kw:kh * 3 + kw + 1, c0:c1]
        return acc * _sigmoid(acc)

    nq = ML_HEADS * ML_QK
    q_ref[...] = conv(qk_m, qk_p, qk_n, wq_ref, 0, nq) * (ML_QK ** -0.5)
    k_ref[...] = conv(qk_m, qk_p, qk_n, wq_ref, nq, 2 * nq)
    nh = HG_HEADS * HG_DK
    hq_ref[...] = conv(qi_m, qi_p, qi_n, wi_ref, 0, nh)
    hv_ref[...] = conv(qi_m, qi_p, qi_n, wi_ref, nh, 2 * nh)


def _conv_call(qk, qi, wq, wi, tpb):
    n = qk.shape[0]
    rpt = TILE // GRID_W
    nrow = n // GRID_W
    cq, ci = qk.shape[1], qi.shape[1]

    def main(i):
        return (i, 0)

    def prev(i):
        return (jnp.maximum(i * rpt - 1, 0), 0)

    def nxt(i):
        return (jnp.minimum((i + 1) * rpt, nrow - 1), 0)

    return pl.pallas_call(
        functools.partial(_conv_kernel, tpb=tpb),
        grid=(n // TILE,),
        in_specs=[
            pl.BlockSpec((TILE, cq), main), pl.BlockSpec((GRID_W, cq), prev), pl.BlockSpec((GRID_W, cq), nxt),
            pl.BlockSpec((TILE, ci), main), pl.BlockSpec((GRID_W, ci), prev), pl.BlockSpec((GRID_W, ci), nxt),
            pl.BlockSpec((9, cq), lambda i: (0, 0)),
            pl.BlockSpec((9, ci), lambda i: (0, 0)),
        ],
        out_specs=[pl.BlockSpec((TILE, s), main) for s in (cq // 2, cq // 2, ci // 2, ci // 2)],
        out_shape=[jax.ShapeDtypeStruct((n, s), F32) for s in (cq // 2, cq // 2, ci // 2, ci // 2)],
        compiler_params=_cparams(1),
        name="conv",
    )(qk, qk, qk, qi, qi, qi, wq.reshape(9, cq), wi.reshape(9, ci))


W_CHUNKS = TILE // CHUNK


def _tile_maps(nct, nt):
    def fwd(b, j):
        return (b * nt + j, 0)

    def bwd(b, j):
        return (b * nt + jnp.where(j < nct, nct - 1 - j, nt + nct - 1 - j), 0)

    return fwd, bwd


def _tri(fwd):
    t = lax.broadcasted_iota(jnp.int32, (CHUNK, CHUNK), 0)
    s = lax.broadcasted_iota(jnp.int32, (CHUNK, CHUNK), 1)
    return (s <= t) if fwd else (s >= t)


def _interleave(gens):
    live = list(gens)
    while live:
        for g in list(live):
            try:
                next(g)
            except StopIteration:
                live.remove(g)


def _mask_dot(mask_bf, x):
    a = x.astype(BF16)
    r = x - a.astype(F32)
    b = r.astype(BF16)
    c = (r - b.astype(F32)).astype(BF16)
    return (_dot(mask_bf, a) + _dot(mask_bf, b)) + _dot(mask_bf, c)


def _mlstm_stages(qf, kf, vf, gf, qb, kb, vb, gb, bias_ref, hf_ref, hb_ref, ct_ref, m_ref):
    assert CHUNK == ML_QK and 2 * ML_QK == LANES
    lane = lax.broadcasted_iota(jnp.int32, (CHUNK, LANES), 1)
    t_id = lax.broadcasted_iota(jnp.int32, (CHUNK, LANES), 0)
    lo = lane < ML_QK
    lo_row = lo[0:1, :]
    s_id = lane % ML_QK
    r128 = lax.broadcasted_iota(jnp.int32, (LANES, LANES), 0)
    c128 = lax.broadcasted_iota(jnp.int32, (LANES, LANES), 1)
    same_head = (r128 // ML_QK) == (c128 // ML_QK)
    ones_blk = jnp.ones((CHUNK, LANES), BF16)
    zeros_va = jnp.zeros((CHUNK, ML_V + LANES), BF16)

    def rep(arr, col):
        return jnp.broadcast_to(arr[:, col:col + 1], (CHUNK, LANES))

    def half_max(x, first):
        sel = lo if x.shape[0] == CHUNK else lo_row
        y = jnp.where(sel if first else jnp.logical_not(sel), x, -jnp.inf)
        return jnp.broadcast_to(jnp.max(y, axis=-1, keepdims=True), x.shape)

    dirs = ((qf, kf, vf, gf, hf_ref), (qb, kb, vb, gb, hb_ref))
    masks = [(jnp.where(_tri(fwd), 1.0, 0.0).astype(BF16), (s_id <= t_id) if fwd else (s_id >= t_id))
             for fwd in (True, False)]
    last_row = (CHUNK - 1, 0)
    orders = (tuple(range(W_CHUNKS)), tuple(range(W_CHUNKS - 1, -1, -1)))
    wv = ML_V + LANES
    shared, local = {}, {}

    def gates_of_chunk(d, c):
        rows = slice(c * CHUNK, (c + 1) * CHUNK)
        g_all = dirs[d][3][rows, :] + bias_ref[...]
        yield
        ls = _log_sigmoid(g_all)
        yield
        bcum = _mask_dot(masks[d][0], ls)
        yield
        g_t = g_all.T
        yield
        shared[(d, c)] = (g_all, bcum, g_t, bcum.T)
        yield

    def chunk_local(d, c, p):
        q_ref, k_ref, v_ref = dirs[d][:3]
        rows = slice(c * CHUNK, (c + 1) * CHUNK)
        g_all, bcum, g_t, b_t = shared[(d, c)]
        mask_p, last = masks[d][1], last_row[d]
        ci = [d * ML_HEADS + 2 * p + e for e in (0, 1)]
        cf = [(2 + d) * ML_HEADS + 2 * p + e for e in (0, 1)]
        b_rep = [rep(bcum, cf[e]) for e in (0, 1)]
        yield
        bp = jnp.where(lo, b_rep[0], b_rep[1])
        ip = jnp.where(lo, rep(g_all, ci[0]), rep(g_all, ci[1]))
        yield
        br = jnp.concatenate([b_t[cf[0]:cf[0] + 1, :], b_t[cf[1]:cf[1] + 1, :]], axis=1)
        ir = jnp.concatenate([g_t[ci[0]:ci[0] + 1, :], g_t[ci[1]:ci[1] + 1, :]], axis=1)
        yield
        dmat = jnp.where(mask_p, bp - br + ir, -jnp.inf)
        yield
        m_rep = [half_max(dmat, True), half_max(dmat, False)]
        yield
        q01 = q_ref[rows, p * LANES:(p + 1) * LANES].astype(BF16)
        k01 = k_ref[rows, p * LANES:(p + 1) * LANES]
        k01b = k01.astype(BF16)
        k_bd = jnp.where(same_head, jnp.concatenate([k01b, k01b], axis=0), jnp.zeros((), BF16))
        yield
        qk = _dot_nt(q01, k_bd)
        yield
        s = qk * jnp.exp(dmat - jnp.where(lo, m_rep[0], m_rep[1]))
        yield
        va = [jnp.concatenate([v_ref[rows, (2 * p + e) * ML_V:(2 * p + e + 1) * ML_V].astype(BF16), ones_blk], axis=1)
              for e in (0, 1)]
        v_bd = jnp.concatenate([jnp.concatenate([va[0], zeros_va], axis=1),
                                jnp.concatenate([zeros_va, va[1]], axis=1)], axis=0)
        yield
        nd_loc = _dot(s.astype(BF16), v_bd)
        yield
        gp = bp[last:last + 1, :]
        e_row = gp - br + ir
        me_rep = [half_max(e_row, True), half_max(e_row, False)]
        yield
        kw = (k01 * jnp.exp(gp - bp + ip - jnp.where(lo_row, me_rep[0], me_rep[1]))).astype(BF16)
        yield
        upd = _dot_tn(kw, jnp.concatenate(va, axis=1))
        yield
        g_tot = [b_rep[e][last:last + 1, :] for e in (0, 1)]
        local[(d, c, p)] = (q01, nd_loc, upd, b_rep, m_rep, g_tot, me_rep)
        yield

    def recurrence(d, p, e):
        h_ref = dirs[d][4]
        h = 2 * p + e
        ci = d * ML_HEADS + h
        m = m_ref[ci:ci + 1, :]
        ct = ct_ref[ci]
        zeros_ct = jnp.zeros((ML_QK, wv), BF16)
        yield
        for c in orders[d]:
            q01, nd_loc, upd, b_rep, m_rep, g_tot, me_rep = local[(d, c, p)]
            ct_b = ct.astype(BF16)
            qc = _dot(q01, jnp.concatenate([ct_b, zeros_ct] if e == 0 else [zeros_ct, ct_b], axis=0))
            yield
            a_inter = b_rep[e] + m
            m_t = jnp.maximum(a_inter, m_rep[e])
            yield
            w_loc = jnp.exp(m_rep[e] - m_t)
            w_int = jnp.exp(a_inter - m_t)
            yield
            num = w_loc * nd_loc[:, e * wv:e * wv + ML_V] + w_int * qc[:, 0:ML_V]
            den = w_loc * nd_loc[:, e * wv + ML_V:(e + 1) * wv] + w_int * qc[:, ML_V:wv]
            yield
            h_ref[c * CHUNK:(c + 1) * CHUNK, h * ML_V:(h + 1) * ML_V] = num / jnp.maximum(jnp.abs(den), jnp.exp(-m_t))
            yield
            m_new = jnp.maximum(g_tot[e] + m, me_rep[e])
            w_c = jnp.exp(g_tot[e] + m - m_new)
            w_u = jnp.exp(me_rep[e] - m_new)
            yield
            ct = (jnp.concatenate([w_c, w_c], axis=1) * ct
                  + jnp.concatenate([w_u, w_u], axis=1) * upd[e * ML_QK:(e + 1) * ML_QK, e * wv:(e + 1) * wv])
            m = m_new
            yield
        ct_ref[ci] = ct
        m_ref[ci:ci + 1, :] = m
        yield

    pairs = range(ML_HEADS // 2)
    return ([gates_of_chunk(d, c) for d in (0, 1) for c in orders[d]],
            [chunk_local(d, c, p) for c in range(W_CHUNKS) for d in (0, 1) for p in pairs],
            [recurrence(d, p, e) for d in (0, 1) for p in pairs for e in (0, 1)])


def _hgrn_stages(qf, vf, ff, qb, vb, fb, fbias_ref, lb_ref, of_ref, ob_ref, st_ref):
    n_sub = CHUNK // SUB
    rows = lax.broadcasted_iota(jnp.int32, (CHUNK, 1), 0)
    t_i = lax.broadcasted_iota(jnp.int32, (CHUNK, CHUNK), 0)
    s_i = lax.broadcasted_iota(jnp.int32, (CHUNK, CHUNK), 1)
    same_sub = (t_i // SUB) == (s_i // SUB)
    lb = lb_ref[...]
    dirs = ((qf, vf, ff, of_ref), (qb, vb, fb, ob_ref))
    masks = [_tri(fwd) for fwd in (True, False)]
    masks_bf = [jnp.where(m, 1.0, 0.0).astype(BF16) for m in masks]
    diag_ok = [jnp.logical_and(same_sub, m) for m in masks]
    orders = (tuple(range(W_CHUNKS)), tuple(range(W_CHUNKS - 1, -1, -1)))
    local = {}

    def chunk_local(d, c):
        q_ref, v_ref, f_ref, _ = dirs[d]
        fwd = d == 0
        last = CHUNK - 1 if fwd else 0
        rs = slice(c * CHUNK, (c + 1) * CHUNK)
        f = lb + (1.0 - lb) * _sigmoid(f_ref[rs, :] + fbias_ref[d:d + 1, :])
        yield
        kk = 1.0 - f
        lf = jnp.log(f)
        yield
        b = _mask_dot(masks_bf[d], lf)
        yield
        q = q_ref[rs, :]
        q_parts, k_parts = [], []
        for p in range(1, n_sub):
            if fwd:
                ref_row, q_ok, k_ok = p * SUB - 1, (rows // SUB) == p, rows < p * SUB
            else:
                ref_row, q_ok, k_ok = p * SUB, (rows // SUB) == p - 1, rows >= p * SUB
            r_p = b[ref_row:ref_row + 1, :]
            q_parts.append(jnp.where(q_ok, q * jnp.exp(jnp.minimum(b - r_p, 0.0)), 0.0))
            yield
            k_parts.append(jnp.where(k_ok, kk * jnp.exp(jnp.minimum(r_p - b, 0.0)), 0.0))
            yield
        mid = SUB // 2 - 1 if fwd else SUB // 2
        c_m = b[mid:mid + 1, :]
        for i in range(1, n_sub):
            c_m = jnp.where(rows >= i * SUB, b[i * SUB + mid:i * SUB + mid + 1, :], c_m)
        yield
        q_d = q * jnp.exp(b - c_m)
        yield
        k_d = kk * jnp.exp(c_m - b)
        yield
        g_row = b[last:last + 1, :]
        q_in = (q * jnp.exp(b)).astype(BF16)
        yield
        k_out = (kk * jnp.exp(g_row - b)).astype(BF16)
        decay = jnp.exp(g_row)
        yield
        for h in range(HG_HEADS):
            sl = slice(h * HG_DK, (h + 1) * HG_DK)
            qc = jnp.concatenate([x[:, sl] for x in q_parts], axis=1).astype(BF16)
            kc = jnp.concatenate([x[:, sl] for x in k_parts], axis=1).astype(BF16)
            yield
            a_off = _dot_nt(qc, kc)
            yield
            a_diag = _dot_nt(q_d[:, sl].astype(BF16), k_d[:, sl].astype(BF16))
            yield
            a = (a_off + jnp.where(diag_ok[d], a_diag, 0.0)).astype(BF16)
            vh = v_ref[rs, sl].astype(BF16)
            yield
            o_loc = _dot(a, vh)
            yield
            local[(d, c, h)] = (o_loc, q_in[:, sl], decay[:, sl], _dot_tn(vh, k_out[:, sl]))
            yield

    def recurrence(d, h):
        o_ref = dirs[d][3]
        st = st_ref[d * HG_HEADS + h]
        yield
        for c in orders[d]:
            o_loc, q_in, decay, upd = local[(d, c, h)]
            inter = _dot_nt(q_in, st.astype(BF16))
            yield
            o_ref[c * CHUNK:(c + 1) * CHUNK, h * HG_DK:(h + 1) * HG_DK] = o_loc + inter
            st = st * decay + upd
            yield
        st_ref[d * HG_HEADS + h] = st
        yield

    return ([chunk_local(d, c) for c in range(W_CHUNKS) for d in (0, 1)],
            [recurrence(d, h) for d in (0, 1) for h in range(HG_HEADS)])


def _mixers_kernel(qf, kf, vf, gf, qb, kb, vb, gb, bias_ref, hqf, hvf, hff, hqb, hvb, hfb, fbias_ref, lb_ref,
                   mlf_ref, mlb_ref, hgf_ref, hgb_ref, ct_ref, m_ref, st_ref):
    @pl.when(pl.program_id(1) == 0)
    def _():
        ct_ref[...] = jnp.zeros_like(ct_ref)
        m_ref[...] = jnp.zeros_like(m_ref)
        st_ref[...] = jnp.zeros_like(st_ref)

    ml_gates, ml_local, ml_rec = _mlstm_stages(qf, kf, vf, gf, qb, kb, vb, gb, bias_ref, mlf_ref, mlb_ref, ct_ref, m_ref)
    hg_local, hg_rec = _hgrn_stages(hqf, hvf, hff, hqb, hvb, hfb, fbias_ref, lb_ref, hgf_ref, hgb_ref, st_ref)
    _interleave(ml_gates)
    _interleave(ml_local + hg_local)
    _interleave(ml_rec + hg_rec)


def _mixers_call(q, k, v, gates, gate_b, hq, hv, ff, fb, f_b, lb, n_batch, nct):
    n = q.shape[0]
    nt = n // TILE // n_batch
    fwd, bwd = _tile_maps(nct, nt)
    dq, dv, w = q.shape[1], v.shape[1], hq.shape[1]
    fix = lambda b, j: (0, 0)
    ml_specs = lambda m: [pl.BlockSpec((TILE, dq), m), pl.BlockSpec((TILE, dq), m),
                          pl.BlockSpec((TILE, dv), m), pl.BlockSpec((TILE, LANES), m)]
    hg_specs = lambda m: [pl.BlockSpec((TILE, w), m)] * 3
    bias = jnp.zeros((1, LANES), F32).at[0, :4 * ML_HEADS].set(gate_b.reshape(-1))
    return pl.pallas_call(
        _mixers_kernel,
        grid=(n_batch, nt),
        in_specs=(ml_specs(fwd) + ml_specs(bwd) + [pl.BlockSpec((1, LANES), fix)]
                  + hg_specs(fwd) + hg_specs(bwd) + [pl.BlockSpec((2, w), fix), pl.BlockSpec((1, w), fix)]),
        out_specs=[pl.BlockSpec((TILE, dv), fwd), pl.BlockSpec((TILE, dv), bwd),
                   pl.BlockSpec((TILE, w), fwd), pl.BlockSpec((TILE, w), bwd)],
        out_shape=[jax.ShapeDtypeStruct((n, dv), F32)] * 2 + [jax.ShapeDtypeStruct((n, w), F32)] * 2,
        scratch_shapes=[pltpu.VMEM((2 * ML_HEADS, ML_QK, ML_V + LANES), F32), pltpu.VMEM((2 * ML_HEADS, LANES), F32),
                        pltpu.VMEM((2 * HG_HEADS, HG_DK, HG_DK), F32)],
        compiler_params=_cparams(2),
        name="mixers",
    )(q, k, v, gates, q, k, v, gates, bias, hq, hv, ff, hq, hv, fb, f_b, lb.reshape(1, w))


def _head_rms(h, g, width):
    parts = []
    for i in range(h.shape[1] // width):
        p = h[:, i * width:(i + 1) * width]
        parts.append(p * lax.rsqrt(jnp.mean(p * p, axis=-1, keepdims=True) + EPS))
    return jnp.concatenate(parts, axis=1) * g


def _out_kernel(x_ref, mlf, mlb, hgf, hgb, o_ref, go_ref, mod_ref, gml_ref, ghg_ref, wout_ref, g2_ref, rw_f32_ref, rb_ref,
                xo_ref, h2_ref, idx_ref, gate_ref, rw_ref, *, tpb, n_batch):
    d = x_ref.shape[1]

    @pl.when(pl.program_id(0) == 0)
    def _():
        w = rw_f32_ref[...]
        w_hi = w.astype(BF16)
        r1 = w - w_hi.astype(F32)
        w_mid = r1.astype(BF16)
        rw_ref[:, 0:LANES] = w_hi
        rw_ref[:, LANES:2 * LANES] = w_mid
        rw_ref[:, 2 * LANES:3 * LANES] = (r1 - w_mid.astype(F32)).astype(BF16)

    def row_group(r0):
        rs = slice(r0, r0 + OUT_ROWS)
        mod = _mod_row(mod_ref, tpb, n_batch, pl.program_id(0) * OUT_TILES + r0 // TILE)
        y_ml = _sigmoid(o_ref[rs, :]) * _head_rms(mlf[rs, :] + mlb[rs, :], gml_ref[...], ML_V)
        yield
        go = go_ref[rs, :]
        y_hg = go * _sigmoid(go) * _head_rms(hgf[rs, :] + hgb[rs, :], ghg_ref[...], HG_DK)
        yield
        y = jnp.concatenate([y_ml, y_hg], axis=1).astype(BF16)
        proj = _dot(y, wout_ref[...])
        yield
        x = x_ref[rs, :] + mod[:, 2 * d:3 * d] * proj
        xo_ref[rs, :] = x
        yield
        h2 = _rms(x, g2_ref[...]) * (1.0 + mod[:, 4 * d:5 * d]) + mod[:, 3 * d:4 * d]
        yield
        _store_slabs(h2_ref, h2, r0 * SLAB)
        yield
        h_hi = h2.astype(BF16)
        r1 = h2 - h_hi.astype(F32)
        h_mid = r1.astype(BF16)
        h_lo = (r1 - h_mid.astype(F32)).astype(BF16)
        yield
        p_hi = _dot(h_hi, rw_ref[...])
        yield
        p_mid = _dot(h_mid, rw_ref[:, :2 * LANES])
        yield
        p_lo = _dot(h_lo, rw_ref[:, :LANES])
        yield
        small = (p_lo + p_mid[:, LANES:]) + p_hi[:, 2 * LANES:]
        vals = ((small + (p_mid[:, :LANES] + p_hi[:, LANES:2 * LANES])) + p_hi[:, :LANES]) + rb_ref[...]
        yield
        lane = lax.broadcasted_iota(jnp.int32, vals.shape, 1)
        lane_f = lane.astype(F32)
        idx_out = jnp.zeros(vals.shape, F32)
        top = []
        for k in range(TOP_K):
            mx = jnp.max(vals, axis=-1, keepdims=True)
            yield
            ix = jnp.min(jnp.where(vals == mx, lane_f, float(LANES)), axis=-1, keepdims=True)
            yield
            top.append(mx)
            idx_out = jnp.where(lane == k, ix, idx_out)
            vals = jnp.where(lane_f == ix, -jnp.inf, vals)
            yield
        ex = [jnp.exp(t - top[0]) for t in top]
        tot = ex[0] + ex[1] + ex[2] + ex[3]
        yield
        gate_out = jnp.zeros(vals.shape, F32)
        for k in range(TOP_K):
            gate_out = jnp.where(lane == k, ex[k] / tot, gate_out)
        idx_ref[rs, :] = idx_out.astype(jnp.int32)
        gate_ref[rs, :] = gate_out
        yield

    _interleave([row_group(r0) for r0 in range(0, OUT_TILES * TILE, OUT_ROWS)])


def _out_call(x, mlf, mlb, hgf, hgb, o, go, mod_l, gml, ghg, wout, g2, rw, rb, tpb, n_batch):
    n, d = x.shape
    w = mlf.shape[1]
    row = lambda i: (i, 0)
    fix = lambda i: (0, 0)
    rw_p = jnp.zeros((d, LANES), F32).at[:, :N_EXPERTS].set(rw)
    rb_p = jnp.full((1, LANES), NEG, F32).at[0, :N_EXPERTS].set(rb)
    rows = OUT_TILES * TILE
    assert n % rows == 0
    return pl.pallas_call(
        functools.partial(_out_kernel, tpb=tpb, n_batch=n_batch),
        grid=(n // rows,),
        in_specs=[pl.BlockSpec((rows, d), row)] + [pl.BlockSpec((rows, w), row)] * 6 + [
            pl.BlockSpec((8, 6 * d), fix), pl.BlockSpec((1, w), fix), pl.BlockSpec((1, w), fix),
            pl.BlockSpec((d, d), fix), pl.BlockSpec((1, d), fix), pl.BlockSpec((d, LANES), fix),
            pl.BlockSpec((1, LANES), fix)],
        out_specs=[pl.BlockSpec((rows, d), row), pl.BlockSpec((rows * SLAB, LANES), row),
                   pl.BlockSpec((rows, LANES), row), pl.BlockSpec((rows, LANES), row)],
        out_shape=[jax.ShapeDtypeStruct((n, d), F32), jax.ShapeDtypeStruct((n * SLAB, LANES), F32),
                   jax.ShapeDtypeStruct((n, LANES), jnp.int32), jax.ShapeDtypeStruct((n, LANES), F32)],
        scratch_shapes=[pltpu.VMEM((d, 3 * LANES), BF16)],
        compiler_params=_cparams(1),
        name="outproj",
    )(x, mlf, mlb, hgf, hgb, o, go, mod_l, gml.reshape(1, w), ghg.reshape(1, w), wout, g2.reshape(1, d), rw_p, rb_p)


def _onehots(idx):
    lane = lax.broadcasted_iota(jnp.int32, idx.shape, 1)
    return [lane == idx[:, k:k + 1] for k in range(TOP_K)]


def _rank_kernel(idx_ref, rank_ref, cnt_ref, carry_ref):
    @pl.when(pl.program_id(0) == 0)
    def _():
        carry_ref[...] = jnp.zeros_like(carry_ref)

    hots = _onehots(idx_ref[...])
    m = jnp.zeros(idx_ref.shape, F32)
    for hk in hots:
        m = m + jnp.where(hk, 1.0, 0.0)
    t_i = lax.broadcasted_iota(jnp.int32, (TILE, TILE), 0)
    s_i = lax.broadcasted_iota(jnp.int32, (TILE, TILE), 1)
    before = _dot((s_i < t_i).astype(BF16), m.astype(BF16)) + carry_ref[...]
    lane = lax.broadcasted_iota(jnp.int32, idx_ref.shape, 1)
    out = jnp.zeros(idx_ref.shape, F32)
    for k, hk in enumerate(hots):
        rk = jnp.sum(jnp.where(hk, before, 0.0), axis=-1, keepdims=True)
        out = jnp.where(lane == k, rk, out)
    rank_ref[...] = out.astype(jnp.int32)
    carry_ref[...] = carry_ref[...] + jnp.sum(m, axis=0, keepdims=True)
    cnt_ref[...] = carry_ref[...]


def _rank_call(idx):
    n = idx.shape[0]
    return pl.pallas_call(
        _rank_kernel,
        grid=(n // TILE,),
        in_specs=[pl.BlockSpec((TILE, LANES), lambda i: (i, 0))],
        out_specs=[pl.BlockSpec((TILE, LANES), lambda i: (i, 0)), pl.BlockSpec((1, LANES), lambda i: (0, 0))],
        out_shape=[jax.ShapeDtypeStruct((n, LANES), jnp.int32), jax.ShapeDtypeStruct((1, LANES), F32)],
        scratch_shapes=[pltpu.VMEM((1, LANES), F32)],
        compiler_params=_cparams(1),
        name="rank",
    )(idx)


def _dest_kernel(idx_ref, rank_ref, cnt_ref, dest_ref, meta_ref, *, n_blocks):
    cnt = cnt_ref[...]
    padded = jnp.floor((cnt + (MOE_BLK - 1)) * (1.0 / MOE_BLK)) * MOE_BLK
    e_i = lax.broadcasted_iota(jnp.int32, (LANES, LANES), 0)
    e_j = lax.broadcasted_iota(jnp.int32, (LANES, LANES), 1)
    pad_start = _dot_hi(jnp.broadcast_to(padded, (8, LANES)), (e_i < e_j).astype(F32))[0:1]
    pad_end = pad_start + padded
    lane = lax.broadcasted_iota(jnp.int32, idx_ref.shape, 1)
    rank = rank_ref[...]
    out = jnp.zeros(idx_ref.shape, jnp.int32)
    for k, hk in enumerate(_onehots(idx_ref[...])):
        st = jnp.sum(jnp.where(hk, pad_start, 0.0), axis=-1, keepdims=True).astype(jnp.int32)
        out = jnp.where(lane == k, st + rank[:, k:k + 1], out)
    dest_ref[...] = out

    @pl.when(pl.program_id(0) == 0)
    def _():
        blk = lax.broadcasted_iota(jnp.int32, meta_ref.shape, 0).astype(F32) * MOE_BLK
        lane_m = lax.broadcasted_iota(jnp.int32, meta_ref.shape, 1)
        done = jnp.where(jnp.logical_and(lane_m < N_EXPERTS, pad_end <= blk), 1.0, 0.0)
        be = jnp.minimum(jnp.sum(done, axis=-1, keepdims=True), N_EXPERTS - 1.0)
        used = jnp.sum(jnp.where(lane_m[0:1] < N_EXPERTS, padded, 0.0), axis=-1, keepdims=True) * (1.0 / MOE_BLK)
        diag = lane_m == lax.broadcasted_iota(jnp.int32, meta_ref.shape, 0)
        end_col = jnp.sum(jnp.where(diag, pad_end, 0.0), axis=-1, keepdims=True)
        pad_col = jnp.sum(jnp.where(diag, padded, 0.0), axis=-1, keepdims=True)
        meta = jnp.where(lane_m == 0, be, jnp.where(lane_m == 1, used, jnp.where(lane_m == 2, end_col, pad_col)))
        meta_ref[...] = jnp.where(lane_m < 4, meta, 0.0).astype(jnp.int32)


def _dest_call(idx, rank, cnt, n_blocks):
    n = idx.shape[0]
    nb_pad = -(-n_blocks // 8) * 8
    row = lambda i: (i, 0)
    return pl.pallas_call(
        functools.partial(_dest_kernel, n_blocks=n_blocks),
        grid=(n // TILE,),
        in_specs=[pl.BlockSpec((TILE, LANES), row), pl.BlockSpec((TILE, LANES), row),
                  pl.BlockSpec((1, LANES), lambda i: (0, 0))],
        out_specs=[pl.BlockSpec((TILE, LANES), row), pl.BlockSpec((nb_pad, LANES), lambda i: (0, 0))],
        out_shape=[jax.ShapeDtypeStruct((n, LANES), jnp.int32), jax.ShapeDtypeStruct((nb_pad, LANES), jnp.int32)],
        compiler_params=_cparams(1),
        name="dest",
    )(idx, rank, cnt)


ROW_UNROLL = 8


def _scatter_kernel(dest_ref, zinfo_ref, h_ref, xb_ref, zbuf, sem):
    blk_rows = MOE_BLK * SLAB

    @pl.when(pl.program_id(0) == 0)
    def _():
        zbuf[...] = jnp.zeros_like(zbuf)
        n_blocks = xb_ref.shape[0] // blk_rows
        used = zinfo_ref[2 * N_EXPERTS]
        for stage in ("start", "wait"):
            for e in range(N_EXPERTS):
                @pl.when(zinfo_ref[2 * e + 1] > 0)
                def _():
                    first = pl.multiple_of((zinfo_ref[2 * e] - MOE_BLK) * SLAB, blk_rows)
                    cp = pltpu.make_async_copy(zbuf, xb_ref.at[pl.ds(first, blk_rows), :], sem)
                    cp.start() if stage == "start" else cp.wait()

                @pl.when(used + e < n_blocks)
                def _():
                    first = pl.multiple_of((used + e) * blk_rows, blk_rows)
                    cp = pltpu.make_async_copy(zbuf, xb_ref.at[pl.ds(first, blk_rows), :], sem)
                    cp.start() if stage == "start" else cp.wait()

    def issue(g, c):
        for u in range(ROW_UNROLL):
            src = h_ref.at[pl.ds(pl.multiple_of((g * ROW_UNROLL + u) * SLAB, SLAB), SLAB), :]
            for k in range(TOP_K):
                dst = pl.multiple_of(dest_ref[g * (ROW_UNROLL * TOP_K) + u * TOP_K + k] * SLAB, SLAB)
                pltpu.make_async_copy(src, xb_ref.at[pl.ds(dst, SLAB), :], sem).start(priority=k % 2)
        return c

    lax.fori_loop(0, TILE // ROW_UNROLL, issue, 0)
    all_rows = xb_ref.at[pl.ds(0, TILE * TOP_K * SLAB), :]
    pltpu.make_async_copy(all_rows, all_rows, sem).wait()


def _scatter_call(dest_flat, zinfo, h2s, n_rows):
    n = h2s.shape[0] // SLAB
    return pl.pallas_call(
        _scatter_kernel,
        grid=(n // TILE,),
        in_specs=[pl.BlockSpec((TILE * TOP_K,), lambda i: (i,), memory_space=pltpu.SMEM),
                  pl.BlockSpec(memory_space=pltpu.SMEM),
                  pl.BlockSpec((TILE * SLAB, LANES), lambda i: (i, 0))],
        out_specs=pl.BlockSpec(memory_space=pl.ANY),
        out_shape=jax.ShapeDtypeStruct((n_rows * SLAB, LANES), F32),
        scratch_shapes=[pltpu.VMEM((MOE_BLK * SLAB, LANES), F32), pltpu.SemaphoreType.DMA(())],
        compiler_params=_cparams(1),
        name="scatter",
    )(dest_flat, zinfo, h2s)


CAST_ROWS = 64


def _cast_rows(src_ref, dst_ref):
    def body(r, c):
        rows = pl.ds(pl.multiple_of(r * CAST_ROWS, CAST_ROWS), CAST_ROWS)
        dst_ref[rows, :] = src_ref[rows, :].astype(BF16)
        return c

    lax.fori_loop(0, src_ref.shape[0] // CAST_ROWS, body, 0)


def _expert_kernel(be_ref, used_ref, nblk_ref, x_ref, wgu_hbm, bgu_ref, wd_hbm, bd_ref, y_ref,
                   wgu_f, wd_f, wgu_bf, wd_bf, slot_ref, sem, *, layer):
    i = pl.program_id(0)
    used = used_ref[0]
    live = i < used
    e = be_ref[i]
    new_expert = jnp.logical_or(i == 0, e != be_ref[jnp.maximum(i - 1, 0)])

    def fetch(expert, slot):
        return (pltpu.make_async_copy(wgu_hbm.at[layer, expert], wgu_f.at[slot], sem.at[0, slot]),
                pltpu.make_async_copy(wd_hbm.at[layer, expert], wd_f.at[slot], sem.at[1, slot]))

    @pl.when(jnp.logical_and(live, i == 0))
    def _():
        slot_ref[0] = 0
        for cp in fetch(e, 0):
            cp.start()

    @pl.when(jnp.logical_not(live))
    def _():
        y_ref[...] = jnp.zeros_like(y_ref)

    @pl.when(jnp.logical_and(live, new_expert))
    def _():
        slot = slot_ref[0]
        for cp in fetch(e, slot):
            cp.wait()
        nxt = i + nblk_ref[e]

        @pl.when(nxt < used)
        def _():
            for cp in fetch(be_ref[nxt], 1 - slot):
                cp.start()

        _cast_rows(wgu_f.at[slot], wgu_bf)
        _cast_rows(wd_f.at[slot], wd_bf)
        slot_ref[0] = 1 - slot

    @pl.when(live)
    def _():
        de = wd_bf.shape[0]
        gu = _dot(_load_slabs(x_ref, MOE_BLK).astype(BF16), wgu_bf[...]) + bgu_ref[...]
        glu = jnp.minimum(gu[:, :de], SWIGLU_LIMIT)
        lin = jnp.clip(gu[:, de:], -SWIGLU_LIMIT, SWIGLU_LIMIT)
        act = glu * _sigmoid(SWIGLU_ALPHA * glu) * (lin + 1.0)
        _store_slabs(y_ref, _dot(act.astype(BF16), wd_bf[...]) + bd_ref[...])


def _expert_call(layer, be, used, nblk, xb, wgu, bgu, wd, bd):
    n_rows = xb.shape[0] // SLAB
    depth, e, d, de2 = wgu.shape
    nb = n_rows // MOE_BLK
    last_live = lambda i, used: jnp.maximum(jnp.minimum(i, used[0] - 1), 0)
    blk = lambda i, be, used, nblk: (last_live(i, used), 0)
    bsel = lambda i, be, used, nblk: (layer, be[last_live(i, used)], 0, 0)
    return pl.pallas_call(
        functools.partial(_expert_kernel, layer=layer),
        grid_spec=pltpu.PrefetchScalarGridSpec(
            num_scalar_prefetch=3,
            grid=(nb,),
            in_specs=[pl.BlockSpec((MOE_BLK * SLAB, LANES), blk),
                      pl.BlockSpec(memory_space=pl.ANY), pl.BlockSpec((None, None, 1, de2), bsel),
                      pl.BlockSpec(memory_space=pl.ANY), pl.BlockSpec((None, None, 1, d), bsel)],
            out_specs=pl.BlockSpec((MOE_BLK * SLAB, LANES), lambda i, be, used, nblk: (i, 0)),
            scratch_shapes=[pltpu.VMEM((2, d, de2), F32), pltpu.VMEM((2, de2 // 2, d), F32),
                            pltpu.VMEM((d, de2), BF16), pltpu.VMEM((de2 // 2, d), BF16),
                            pltpu.SMEM((1,), jnp.int32), pltpu.SemaphoreType.DMA((2, 2))],
        ),
        out_shape=jax.ShapeDtypeStruct((n_rows * SLAB, LANES), F32),
        compiler_params=_cparams(1, vmem=EXPERT_VMEM_LIMIT),
        name="expert",
    )(be, used, nblk, xb, wgu, bgu.reshape(depth, e, 1, de2), wd, bd.reshape(depth, e, 1, d))


def _combine_kernel(dest_ref, dest_next_ref, x_ref, gate_ref, mod_ref, fg_ref, yb_ref, o_ref, buf, sem,
                    *, tpb, n_batch, final):
    i = pl.program_id(0)
    slot = i % 2

    def gather(idx_ref, to_slot):
        def issue(g, c):
            for u in range(ROW_UNROLL):
                rows = pl.ds(pl.multiple_of((g * ROW_UNROLL + u) * SLAB, SLAB), SLAB)
                for k in range(TOP_K):
                    src = pl.multiple_of(idx_ref[g * (ROW_UNROLL * TOP_K) + u * TOP_K + k] * SLAB, SLAB)
                    pltpu.make_async_copy(yb_ref.at[pl.ds(src, SLAB), :], buf.at[to_slot, k, rows, :],
                                          sem.at[to_slot]).start(priority=k % 2)
            return c

        lax.fori_loop(0, TILE // ROW_UNROLL, issue, 0)

    @pl.when(i == 0)
    def _():
        gather(dest_ref, 0)

    @pl.when(i + 1 < pl.num_programs(0))
    def _():
        gather(dest_next_ref, 1 - slot)

    cur = buf.at[slot]
    pltpu.make_async_copy(cur, cur, sem.at[slot]).wait()
    d = x_ref.shape[1]
    mod = _mod_row(mod_ref, tpb, n_batch)
    gates = gate_ref[...]
    f = gates[:, 0:1] * _load_slabs(cur.at[0], TILE)
    for k in range(1, TOP_K):
        f = f + gates[:, k:k + 1] * _load_slabs(cur.at[k], TILE)
    x = x_ref[...] + mod[:, 5 * d:6 * d] * f
    o_ref[...] = _rms(x, fg_ref[...]) if final else x


def _combine_call(dest_flat, x, gates, mod_l, fg, yb, tpb, n_batch, final):
    n, d = x.shape
    if final:
        out_rows = n - n_batch * TILE
        omap = lambda i: ((i // tpb) * (tpb - 1) + jnp.maximum(i % tpb - 1, 0), 0)
    else:
        out_rows = n
        omap = lambda i: (i, 0)
    nt = n // TILE
    return pl.pallas_call(
        functools.partial(_combine_kernel, tpb=tpb, n_batch=n_batch, final=final),
        grid=(nt,),
        in_specs=[pl.BlockSpec((TILE * TOP_K,), lambda i: (i,), memory_space=pltpu.SMEM),
                  pl.BlockSpec((TILE * TOP_K,), lambda i: (jnp.minimum(i + 1, nt - 1),), memory_space=pltpu.SMEM),
                  pl.BlockSpec((TILE, d), lambda i: (i, 0)),
                  pl.BlockSpec((TILE, LANES), lambda i: (i, 0)),
                  pl.BlockSpec((8, 6 * d), lambda i: (0, 0)),
                  pl.BlockSpec((1, d), lambda i: (0, 0)),
                  pl.BlockSpec(memory_space=pl.ANY)],
        out_specs=pl.BlockSpec((TILE, d), omap),
        out_shape=jax.ShapeDtypeStruct((out_rows, d), F32),
        scratch_shapes=[pltpu.VMEM((2, TOP_K, TILE * SLAB, LANES), F32), pltpu.SemaphoreType.DMA((2,))],
        compiler_params=_cparams(1),
        name="combine",
    )(dest_flat, dest_flat, x, gates, mod_l, fg.reshape(1, d), yb)


def kernel(x, c, ctx, c_ctx, w_ada, b_ada, norm1_g, w_in, mlstm_conv, mlstm_gate_b, mlstm_norm_g, hgrn_conv, hgrn_f_b,
           hgrn_lb_raw, hgrn_norm_g, w_out, norm2_g, router_w, router_b, w_gu, b_gu, w_down, b_down, final_g):
    n_batch, seq, d = x.shape
    ctx_len = ctx.shape[1]
    depth = w_ada.shape[0]
    assert ctx_len == TILE and seq % TILE == 0 and n_batch + 1 <= 8 and d == SLAB * LANES
    tpb = (ctx_len + seq) // TILE
    nct = ctx_len // TILE
    n = n_batch * (ctx_len + seq)
    n_blocks = -(-(n * TOP_K) // MOE_BLK) + N_EXPERTS
    n_rows = n_blocks * MOE_BLK

    xa = jnp.concatenate([ctx, x], axis=1).reshape(n, d)
    cond = jnp.zeros((8, d), F32).at[:n_batch].set(c).at[n_batch].set(c_ctx)
    mod = _ada_call(cond, w_ada, b_ada)

    lb_w = jax.nn.softmax(hgrn_lb_raw.astype(F32), axis=0)
    lower = jnp.cumsum(lb_w, axis=0) - lb_w[0]

    n_gate = 4 * ML_HEADS
    g0 = 3 * 512
    w_in_p = jnp.concatenate([w_in[:, :, :g0], jnp.pad(w_in[:, :, g0:g0 + n_gate], ((0, 0), (0, 0), (0, LANES - n_gate))),
                              w_in[:, :, g0 + n_gate:]], axis=2).astype(BF16)
    w_out_b = w_out.astype(BF16)

    for l in range(depth):
        last = l == depth - 1
        qk, v, o, gates, qi, ff, fb, go = _inproj_call(xa, mod[l], norm1_g[l], w_in_p[l], tpb, n_batch)
        q, k, hq, hv = _conv_call(qk, qi, mlstm_conv[l], hgrn_conv[l], tpb)
        mlf, mlb, hgf, hgb = _mixers_call(q, k, v, gates, mlstm_gate_b[l], hq, hv, ff, fb, hgrn_f_b[l], lower[l],
                                          n_batch, nct)
        xa, h2, idx, gate = _out_call(xa, mlf, mlb, hgf, hgb, o, go, mod[l], mlstm_norm_g[l], hgrn_norm_g[l], w_out_b[l],
                                      norm2_g[l], router_w[l], router_b[l], tpb, n_batch)
        rank, cnt = _rank_call(idx)
        dest, meta = _dest_call(idx, rank, cnt, n_blocks)
        dest_flat = dest[:, :TOP_K].reshape(-1)
        zinfo = jnp.concatenate([meta[:N_EXPERTS, 2:4].reshape(-1), meta[0:1, 1]])
        xb = _scatter_call(dest_flat, zinfo, h2, n_rows)
        nblk = meta[:N_EXPERTS, 3] // MOE_BLK
        yb = _expert_call(l, meta[:n_blocks, 0], meta[0:1, 1], nblk, xb, w_gu, b_gu, w_down, b_down)
        xa = _combine_call(dest_flat, xa, gate, mod[l], final_g, yb, tpb, n_batch, last)
    return xa.reshape(n_batch, seq, d)
```

```python
import functools

import jax
import jax.numpy as jnp
from jax import lax
from jax.experimental import pallas as pl
from jax.experimental.pallas import tpu as pltpu

F32 = jnp.float32
BF16 = jnp.bfloat16
HIGHEST = lax.Precision.HIGHEST

GRID_W = 64
CHUNK = 64
ML_HEADS = 4
ML_QK = 64
ML_V = 128
HG_HEADS = 4
HG_DK = 128
N_EXPERTS = 32
TOP_K = 4
SWIGLU_LIMIT = 7.0
SWIGLU_ALPHA = 1.702
EPS = 1e-6

TILE = 256
OUT_TILES = 2
OUT_ROWS = 128
MOE_BLK = 256
LANES = 128
SUB = 16
NEG = -1e30
VMEM_LIMIT = 48 * 1024 * 1024
EXPERT_VMEM_LIMIT = 56 * 1024 * 1024


def _cparams(n_axes, vmem=VMEM_LIMIT):
    return pltpu.CompilerParams(dimension_semantics=("arbitrary",) * n_axes, vmem_limit_bytes=vmem)


def _dot(a, b):
    return jnp.dot(a, b, preferred_element_type=F32)


def _dot_nt(a, b):
    return lax.dot_general(a, b, (((1,), (1,)), ((), ())), preferred_element_type=F32)


def _dot_tn(a, b):
    return lax.dot_general(a, b, (((0,), (0,)), ((), ())), preferred_element_type=F32)


def _dot_hi(a, b):
    return jnp.dot(a, b, precision=HIGHEST, preferred_element_type=F32)


def _rms(x, g):
    return x * lax.rsqrt(jnp.mean(x * x, axis=-1, keepdims=True) + EPS) * g


def _sigmoid(x):
    return 1.0 / (1.0 + jnp.exp(-x))


def _log_sigmoid(x):
    return jnp.minimum(x, 0.0) - jnp.log(1.0 + jnp.exp(-jnp.abs(x)))


SLAB = 8


def _store_slabs(ref, x, base=0):
    rows = x.shape[0]
    for s in range(SLAB):
        ref[pl.ds(base + s, rows, stride=SLAB), :] = x[:, s * LANES:(s + 1) * LANES]


def _load_slabs(ref, rows):
    return jnp.concatenate([ref[pl.ds(s, rows, stride=SLAB), :] for s in range(SLAB)], axis=1)


def _ada_kernel(cond_ref, w_ref, b_ref, o_ref):
    c = cond_ref[...]
    o_ref[...] = _dot_hi(c * _sigmoid(c), w_ref[...]) + b_ref[...]


def _ada_call(cond, w_ada, b_ada):
    depth, d, n6 = w_ada.shape
    tn = 1024
    return pl.pallas_call(
        _ada_kernel,
        grid=(depth, n6 // tn),
        in_specs=[
            pl.BlockSpec((8, d), lambda l, j: (0, 0)),
            pl.BlockSpec((None, d, tn), lambda l, j: (l, 0, j)),
            pl.BlockSpec((None, 1, tn), lambda l, j: (l, 0, j)),
        ],
        out_specs=pl.BlockSpec((None, 8, tn), lambda l, j: (l, 0, j)),
        out_shape=jax.ShapeDtypeStruct((depth, 8, n6), F32),
        compiler_params=_cparams(2),
        name="ada",
    )(cond, w_ada, b_ada.reshape(depth, 1, n6))


def _mod_row(mod_ref, tpb, n_batch, tile=None):
    i = pl.program_id(0) if tile is None else tile
    row = jnp.where(i % tpb == 0, n_batch, i // tpb)
    return mod_ref[pl.ds(row, 1), :]


IN_SEGS = (512, 512, 512, 128, 1024, 512, 512, 512)


IN_OFFS = tuple(sum(IN_SEGS[:i]) for i in range(len(IN_SEGS)))
CONV_OUT = (256, 256, 512, 512)


def _inconv_kernel(x_m, x_p, x_n, mod_ref, g_ref, w_ref, wq_ref, wi_ref,
                   q_ref, k_ref, hq_ref, hv_ref, v_ref, o_ref, gates_ref, ff_ref, fb_ref, go_ref, *, tpb, n_batch):
    d = x_m.shape[1]
    mod = _mod_row(mod_ref, tpb, n_batch)
    norm = lambda ref: (_rms(ref[...], g_ref[...]) * (1.0 + mod[:, d:2 * d]) + mod[:, 0:d]).astype(BF16)
    h_main = norm(x_m)
    h_ext = jnp.concatenate([norm(x_p), h_main, norm(x_n)], axis=0)

    r = lax.broadcasted_iota(jnp.int32, (TILE, 1), 0)
    j = jnp.zeros((TILE, 1), jnp.int32) + pl.program_id(0) % tpb
    is_ctx = j == 0
    col = r % GRID_W
    ok_l = jnp.logical_or(col != 0, jnp.logical_and(is_ctx, r != 0))
    ok_r = jnp.logical_or(col != GRID_W - 1, jnp.logical_and(is_ctx, r != TILE - 1))
    ok_u = jnp.logical_and(jnp.logical_not(is_ctx), jnp.logical_or(j != 1, r >= GRID_W))
    ok_d = jnp.logical_and(jnp.logical_not(is_ctx), jnp.logical_or(j != tpb - 1, r < TILE - GRID_W))
    col_ok = (ok_l, None, ok_r)
    row_ok = (ok_u, None, ok_d)
    n_ext = TILE + 2 * GRID_W

    def conv(out_ref, w_off, cw_ref, c0, c1, scale, start):
        for _ in range(start):
            yield
        ext = _dot(h_ext, w_ref[:, w_off + c0:w_off + c1])
        yield
        shifted = (pltpu.roll(ext, 1, 0), ext, pltpu.roll(ext, n_ext - 1, 0))
        yield
        acc = jnp.zeros((TILE, c1 - c0), F32)
        for kh in range(3):
            for kw in range(3):
                tap = shifted[kw][kh * GRID_W:kh * GRID_W + TILE]
                ok = None
                for m in (row_ok[kh], col_ok[kw]):
                    if m is not None:
                        ok = m if ok is None else jnp.logical_and(ok, m)
                if ok is not None:
                    tap = jnp.where(ok, tap, 0.0)
                acc = acc + tap * cw_ref[kh * 3 + kw:kh * 3 + kw + 1, c0:c1]
                yield
        y = acc * _sigmoid(acc)
        out_ref[...] = y if scale is None else y * scale
        yield

    def plain():
        for out_ref, seg in ((v_ref, 1), (o_ref, 2), (gates_ref, 3), (ff_ref, 5), (fb_ref, 6), (go_ref, 7)):
            out_ref[...] = _dot(h_main, w_ref[:, IN_OFFS[seg]:IN_OFFS[seg] + IN_SEGS[seg]])
            yield
            yield
            yield

    nq = ML_HEADS * ML_QK
    nh = HG_HEADS * HG_DK
    _interleave([conv(q_ref, IN_OFFS[0], wq_ref, 0, nq, ML_QK ** -0.5, 0), conv(k_ref, IN_OFFS[0], wq_ref, nq, 2 * nq, None, 4),
                 conv(hq_ref, IN_OFFS[4], wi_ref, 0, nh, None, 8), conv(hv_ref, IN_OFFS[4], wi_ref, nh, 2 * nh, None, 12),
                 plain()])


def _inconv_call(x, mod_l, g, w_p, wq, wi, tpb, n_batch):
    n, d = x.shape
    nw = w_p.shape[1]
    rpt = TILE // GRID_W
    nrow = n // GRID_W
    main = lambda i: (i, 0)
    prev = lambda i: (jnp.maximum(i * rpt - 1, 0), 0)
    nxt = lambda i: (jnp.minimum((i + 1) * rpt, nrow - 1), 0)
    fix = lambda i: (0, 0)
    out_w = CONV_OUT + tuple(IN_SEGS[s] for s in (1, 2, 3, 5, 6, 7))
    return pl.pallas_call(
        functools.partial(_inconv_kernel, tpb=tpb, n_batch=n_batch),
        grid=(n // TILE,),
        in_specs=[pl.BlockSpec((TILE, d), main), pl.BlockSpec((GRID_W, d), prev), pl.BlockSpec((GRID_W, d), nxt),
                  pl.BlockSpec((8, 6 * d), fix), pl.BlockSpec((1, d), fix), pl.BlockSpec((d, nw), fix),
                  pl.BlockSpec((9, IN_SEGS[0]), fix), pl.BlockSpec((9, IN_SEGS[4]), fix)],
        out_specs=[pl.BlockSpec((TILE, s), main) for s in out_w],
        out_shape=[jax.ShapeDtypeStruct((n, s), F32) for s in out_w],
        compiler_params=_cparams(1),
        name="inconv",
    )(x, x, x, mod_l, g.reshape(1, d), w_p, wq.reshape(9, IN_SEGS[0]), wi.reshape(9, IN_SEGS[4]))


W_CHUNKS = TILE // CHUNK


def _tile_maps(nct, nt):
    def fwd(b, j):
        return (b * nt + j, 0)

    def bwd(b, j):
        return (b * nt + jnp.where(j < nct, nct - 1 - j, nt + nct - 1 - j), 0)

    return fwd, bwd


def _tri(fwd):
    t = lax.broadcasted_iota(jnp.int32, (CHUNK, CHUNK), 0)
    s = lax.broadcasted_iota(jnp.int32, (CHUNK, CHUNK), 1)
    return (s <= t) if fwd else (s >= t)


def _interleave(gens):
    live = list(gens)
    while live:
        for g in list(live):
            try:
                next(g)
            except StopIteration:
                live.remove(g)


def _mask_dot(mask_bf, x):
    a = x.astype(BF16)
    r = x - a.astype(F32)
    b = r.astype(BF16)
    c = (r - b.astype(F32)).astype(BF16)
    return (_dot(mask_bf, a) + _dot(mask_bf, b)) + _dot(mask_bf, c)


def _mlstm_stages(qf, kf, vf, gf, qb, kb, vb, gb, bias_ref, hf_ref, hb_ref, ct_ref, m_ref):
    assert CHUNK == ML_QK and 2 * ML_QK == LANES
    lane = lax.broadcasted_iota(jnp.int32, (CHUNK, LANES), 1)
    t_id = lax.broadcasted_iota(jnp.int32, (CHUNK, LANES), 0)
    lo = lane < ML_QK
    lo_row = lo[0:1, :]
    s_id = lane % ML_QK
    r128 = lax.broadcasted_iota(jnp.int32, (LANES, LANES), 0)
    c128 = lax.broadcasted_iota(jnp.int32, (LANES, LANES), 1)
    same_head = (r128 // ML_QK) == (c128 // ML_QK)
    ones_blk = jnp.ones((CHUNK, LANES), BF16)
    zeros_va = jnp.zeros((CHUNK, ML_V + LANES), BF16)

    def rep(arr, col):
        return jnp.broadcast_to(arr[:, col:col + 1], (CHUNK, LANES))

    def half_max(x, first):
        sel = lo if x.shape[0] == CHUNK else lo_row
        y = jnp.where(sel if first else jnp.logical_not(sel), x, -jnp.inf)
        return jnp.broadcast_to(jnp.max(y, axis=-1, keepdims=True), x.shape)

    dirs = ((qf, kf, vf, gf, hf_ref), (qb, kb, vb, gb, hb_ref))
    masks = [(jnp.where(_tri(fwd), 1.0, 0.0).astype(BF16), (s_id <= t_id) if fwd else (s_id >= t_id))
             for fwd in (True, False)]
    last_row = (CHUNK - 1, 0)
    orders = (tuple(range(W_CHUNKS)), tuple(range(W_CHUNKS - 1, -1, -1)))
    wv = ML_V + LANES
    shared, local = {}, {}

    def gates_of_chunk(d, c):
        rows = slice(c * CHUNK, (c + 1) * CHUNK)
        g_all = dirs[d][3][rows, :] + bias_ref[...]
        yield
        ls = _log_sigmoid(g_all)
        yield
        bcum = _mask_dot(masks[d][0], ls)
        yield
        g_t = g_all.T
        yield
        shared[(d, c)] = (g_all, bcum, g_t, bcum.T)
        yield

    def chunk_local(d, c, p):
        q_ref, k_ref, v_ref = dirs[d][:3]
        rows = slice(c * CHUNK, (c + 1) * CHUNK)
        g_all, bcum, g_t, b_t = shared[(d, c)]
        mask_p, last = masks[d][1], last_row[d]
        ci = [d * ML_HEADS + 2 * p + e for e in (0, 1)]
        cf = [(2 + d) * ML_HEADS + 2 * p + e for e in (0, 1)]
        b_rep = [rep(bcum, cf[e]) for e in (0, 1)]
        yield
        bp = jnp.where(lo, b_rep[0], b_rep[1])
        ip = jnp.where(lo, rep(g_all, ci[0]), rep(g_all, ci[1]))
        yield
        br = jnp.concatenate([b_t[cf[0]:cf[0] + 1, :], b_t[cf[1]:cf[1] + 1, :]], axis=1)
        ir = jnp.concatenate([g_t[ci[0]:ci[0] + 1, :], g_t[ci[1]:ci[1] + 1, :]], axis=1)
        yield
        dmat = jnp.where(mask_p, bp - br + ir, -jnp.inf)
        yield
        m_rep = [half_max(dmat, True), half_max(dmat, False)]
        yield
        q01 = q_ref[rows, p * LANES:(p + 1) * LANES].astype(BF16)
        k01 = k_ref[rows, p * LANES:(p + 1) * LANES]
        k01b = k01.astype(BF16)
        k_bd = jnp.where(same_head, jnp.concatenate([k01b, k01b], axis=0), jnp.zeros((), BF16))
        yield
        qk = _dot_nt(q01, k_bd)
        yield
        s = qk * jnp.exp(dmat - jnp.where(lo, m_rep[0], m_rep[1]))
        yield
        va = [jnp.concatenate([v_ref[rows, (2 * p + e) * ML_V:(2 * p + e + 1) * ML_V].astype(BF16), ones_blk], axis=1)
              for e in (0, 1)]
        v_bd = jnp.concatenate([jnp.concatenate([va[0], zeros_va], axis=1),
                                jnp.concatenate([zeros_va, va[1]], axis=1)], axis=0)
        yield
        nd_loc = _dot(s.astype(BF16), v_bd)
        yield
        gp = bp[last:last + 1, :]
        e_row = gp - br + ir
        me_rep = [half_max(e_row, True), half_max(e_row, False)]
        yield
        kw = (k01 * jnp.exp(gp - bp + ip - jnp.where(lo_row, me_rep[0], me_rep[1]))).astype(BF16)
        yield
        upd = _dot_tn(kw, jnp.concatenate(va, axis=1))
        yield
        g_tot = [b_rep[e][last:last + 1, :] for e in (0, 1)]
        local[(d, c, p)] = (q01, nd_loc, upd, b_rep, m_rep, g_tot, me_rep)
        yield

    def recurrence(d, p, e):
        h_ref = dirs[d][4]
        h = 2 * p + e
        ci = d * ML_HEADS + h
        m = m_ref[ci:ci + 1, :]
        ct = ct_ref[ci]
        zeros_ct = jnp.zeros((ML_QK, wv), BF16)
        yield
        for c in orders[d]:
            q01, nd_loc, upd, b_rep, m_rep, g_tot, me_rep = local[(d, c, p)]
            ct_b = ct.astype(BF16)
            qc = _dot(q01, jnp.concatenate([ct_b, zeros_ct] if e == 0 else [zeros_ct, ct_b], axis=0))
            yield
            a_inter = b_rep[e] + m
            m_t = jnp.maximum(a_inter, m_rep[e])
            yield
            w_loc = jnp.exp(m_rep[e] - m_t)
            w_int = jnp.exp(a_inter - m_t)
            yield
            num = w_loc * nd_loc[:, e * wv:e * wv + ML_V] + w_int * qc[:, 0:ML_V]
            den = w_loc * nd_loc[:, e * wv + ML_V:(e + 1) * wv] + w_int * qc[:, ML_V:wv]
            yield
            h_ref[c * CHUNK:(c + 1) * CHUNK, h * ML_V:(h + 1) * ML_V] = num / jnp.maximum(jnp.abs(den), jnp.exp(-m_t))
            yield
            m_new = jnp.maximum(g_tot[e] + m, me_rep[e])
            w_c = jnp.exp(g_tot[e] + m - m_new)
            w_u = jnp.exp(me_rep[e] - m_new)
            yield
            ct = (jnp.concatenate([w_c, w_c], axis=1) * ct
                  + jnp.concatenate([w_u, w_u], axis=1) * upd[e * ML_QK:(e + 1) * ML_QK, e * wv:(e + 1) * wv])
            m = m_new
            yield
        ct_ref[ci] = ct
        m_ref[ci:ci + 1, :] = m
        yield

    pairs = range(ML_HEADS // 2)
    return ([gates_of_chunk(d, c) for d in (0, 1) for c in orders[d]],
            [chunk_local(d, c, p) for c in range(W_CHUNKS) for d in (0, 1) for p in pairs],
            [recurrence(d, p, e) for d in (0, 1) for p in pairs for e in (0, 1)])


def _hgrn_stages(qf, vf, ff, qb, vb, fb, fbias_ref, lb_ref, of_ref, ob_ref, st_ref):
    n_sub = CHUNK // SUB
    rows = lax.broadcasted_iota(jnp.int32, (CHUNK, 1), 0)
    t_i = lax.broadcasted_iota(jnp.int32, (CHUNK, CHUNK), 0)
    s_i = lax.broadcasted_iota(jnp.int32, (CHUNK, CHUNK), 1)
    same_sub = (t_i // SUB) == (s_i // SUB)
    lb = lb_ref[...]
    dirs = ((qf, vf, ff, of_ref), (qb, vb, fb, ob_ref))
    masks = [_tri(fwd) for fwd in (True, False)]
    masks_bf = [jnp.where(m, 1.0, 0.0).astype(BF16) for m in masks]
    diag_ok = [jnp.logical_and(same_sub, m) for m in masks]
    orders = (tuple(range(W_CHUNKS)), tuple(range(W_CHUNKS - 1, -1, -1)))
    local = {}

    def chunk_local(d, c):
        q_ref, v_ref, f_ref, _ = dirs[d]
        fwd = d == 0
        last = CHUNK - 1 if fwd else 0
        rs = slice(c * CHUNK, (c + 1) * CHUNK)
        f = lb + (1.0 - lb) * _sigmoid(f_ref[rs, :] + fbias_ref[d:d + 1, :])
        yield
        kk = 1.0 - f
        lf = jnp.log(f)
        yield
        b = _mask_dot(masks_bf[d], lf)
        yield
        q = q_ref[rs, :]
        q_parts, k_parts = [], []
        for p in range(1, n_sub):
            if fwd:
                ref_row, q_ok, k_ok = p * SUB - 1, (rows // SUB) == p, rows < p * SUB
            else:
                ref_row, q_ok, k_ok = p * SUB, (rows // SUB) == p - 1, rows >= p * SUB
            r_p = b[ref_row:ref_row + 1, :]
            q_parts.append(jnp.where(q_ok, q * jnp.exp(jnp.minimum(b - r_p, 0.0)), 0.0))
            yield
            k_parts.append(jnp.where(k_ok, kk * jnp.exp(jnp.minimum(r_p - b, 0.0)), 0.0))
            yield
        mid = SUB // 2 - 1 if fwd else SUB // 2
        c_m = b[mid:mid + 1, :]
        for i in range(1, n_sub):
            c_m = jnp.where(rows >= i * SUB, b[i * SUB + mid:i * SUB + mid + 1, :], c_m)
        yield
        q_d = q * jnp.exp(b - c_m)
        yield
        k_d = kk * jnp.exp(c_m - b)
        yield
        g_row = b[last:last + 1, :]
        q_in = (q * jnp.exp(b)).astype(BF16)
        yield
        k_out = (kk * jnp.exp(g_row - b)).astype(BF16)
        decay = jnp.exp(g_row)
        yield
        for h in range(HG_HEADS):
            sl = slice(h * HG_DK, (h + 1) * HG_DK)
            qc = jnp.concatenate([x[:, sl] for x in q_parts], axis=1).astype(BF16)
            kc = jnp.concatenate([x[:, sl] for x in k_parts], axis=1).astype(BF16)
            yield
            a_off = _dot_nt(qc, kc)
            yield
            a_diag = _dot_nt(q_d[:, sl].astype(BF16), k_d[:, sl].astype(BF16))
            yield
            a = (a_off + jnp.where(diag_ok[d], a_diag, 0.0)).astype(BF16)
            vh = v_ref[rs, sl].astype(BF16)
            yield
            o_loc = _dot(a, vh)
            yield
            local[(d, c, h)] = (o_loc, q_in[:, sl], decay[:, sl], _dot_tn(vh, k_out[:, sl]))
            yield

    def recurrence(d, h):
        o_ref = dirs[d][3]
        st = st_ref[d * HG_HEADS + h]
        yield
        for c in orders[d]:
            o_loc, q_in, decay, upd = local[(d, c, h)]
            inter = _dot_nt(q_in, st.astype(BF16))
            yield
            o_ref[c * CHUNK:(c + 1) * CHUNK, h * HG_DK:(h + 1) * HG_DK] = o_loc + inter
            st = st * decay + upd
            yield
        st_ref[d * HG_HEADS + h] = st
        yield

    return ([chunk_local(d, c) for c in range(W_CHUNKS) for d in (0, 1)],
            [recurrence(d, h) for d in (0, 1) for h in range(HG_HEADS)])


def _mixers_kernel(qf, kf, vf, gf, qb, kb, vb, gb, bias_ref, hqf, hvf, hff, hqb, hvb, hfb, fbias_ref, lb_ref,
                   mlf_ref, mlb_ref, hgf_ref, hgb_ref, ct_ref, m_ref, st_ref):
    @pl.when(pl.program_id(1) == 0)
    def _():
        ct_ref[...] = jnp.zeros_like(ct_ref)
        m_ref[...] = jnp.zeros_like(m_ref)
        st_ref[...] = jnp.zeros_like(st_ref)

    ml_gates, ml_local, ml_rec = _mlstm_stages(qf, kf, vf, gf, qb, kb, vb, gb, bias_ref, mlf_ref, mlb_ref, ct_ref, m_ref)
    hg_local, hg_rec = _hgrn_stages(hqf, hvf, hff, hqb, hvb, hfb, fbias_ref, lb_ref, hgf_ref, hgb_ref, st_ref)
    _interleave(ml_gates)
    _interleave(ml_local + hg_local)
    _interleave(ml_rec + hg_rec)


def _mixers_call(q, k, v, gates, gate_b, hq, hv, ff, fb, f_b, lb, n_batch, nct):
    n = q.shape[0]
    nt = n // TILE // n_batch
    fwd, bwd = _tile_maps(nct, nt)
    dq, dv, w = q.shape[1], v.shape[1], hq.shape[1]
    fix = lambda b, j: (0, 0)
    ml_specs = lambda m: [pl.BlockSpec((TILE, dq), m), pl.BlockSpec((TILE, dq), m),
                          pl.BlockSpec((TILE, dv), m), pl.BlockSpec((TILE, LANES), m)]
    hg_specs = lambda m: [pl.BlockSpec((TILE, w), m)] * 3
    bias = jnp.zeros((1, LANES), F32).at[0, :4 * ML_HEADS].set(gate_b.reshape(-1))
    return pl.pallas_call(
        _mixers_kernel,
        grid=(n_batch, nt),
        in_specs=(ml_specs(fwd) + ml_specs(bwd) + [pl.BlockSpec((1, LANES), fix)]
                  + hg_specs(fwd) + hg_specs(bwd) + [pl.BlockSpec((2, w), fix), pl.BlockSpec((1, w), fix)]),
        out_specs=[pl.BlockSpec((TILE, dv), fwd), pl.BlockSpec((TILE, dv), bwd),
                   pl.BlockSpec((TILE, w), fwd), pl.BlockSpec((TILE, w), bwd)],
        out_shape=[jax.ShapeDtypeStruct((n, dv), F32)] * 2 + [jax.ShapeDtypeStruct((n, w), F32)] * 2,
        scratch_shapes=[pltpu.VMEM((2 * ML_HEADS, ML_QK, ML_V + LANES), F32), pltpu.VMEM((2 * ML_HEADS, LANES), F32),
                        pltpu.VMEM((2 * HG_HEADS, HG_DK, HG_DK), F32)],
        compiler_params=_cparams(2),
        name="mixers",
    )(q, k, v, gates, q, k, v, gates, bias, hq, hv, ff, hq, hv, fb, f_b, lb.reshape(1, w))


def _head_rms(h, g, width):
    parts = []
    for i in range(h.shape[1] // width):
        p = h[:, i * width:(i + 1) * width]
        parts.append(p * lax.rsqrt(jnp.mean(p * p, axis=-1, keepdims=True) + EPS))
    return jnp.concatenate(parts, axis=1) * g


def _out_kernel(x_ref, mlf, mlb, hgf, hgb, o_ref, go_ref, mod_ref, gml_ref, ghg_ref, wout_ref, g2_ref, rw_f32_ref, rb_ref,
                xo_ref, h2_ref, idx_ref, gate_ref, rw_ref, *, tpb, n_batch):
    d = x_ref.shape[1]

    @pl.when(pl.program_id(0) == 0)
    def _():
        w = rw_f32_ref[...]
        w_hi = w.astype(BF16)
        r1 = w - w_hi.astype(F32)
        w_mid = r1.astype(BF16)
        rw_ref[:, 0:LANES] = w_hi
        rw_ref[:, LANES:2 * LANES] = w_mid
        rw_ref[:, 2 * LANES:3 * LANES] = (r1 - w_mid.astype(F32)).astype(BF16)

    def row_group(r0):
        rs = slice(r0, r0 + OUT_ROWS)
        mod = _mod_row(mod_ref, tpb, n_batch, pl.program_id(0) * OUT_TILES + r0 // TILE)
        y_ml = _sigmoid(o_ref[rs, :]) * _head_rms(mlf[rs, :] + mlb[rs, :], gml_ref[...], ML_V)
        yield
        go = go_ref[rs, :]
        y_hg = go * _sigmoid(go) * _head_rms(hgf[rs, :] + hgb[rs, :], ghg_ref[...], HG_DK)
        yield
        y = jnp.concatenate([y_ml, y_hg], axis=1).astype(BF16)
        proj = _dot(y, wout_ref[...])
        yield
        x = x_ref[rs, :] + mod[:, 2 * d:3 * d] * proj
        xo_ref[rs, :] = x
        yield
        h2 = _rms(x, g2_ref[...]) * (1.0 + mod[:, 4 * d:5 * d]) + mod[:, 3 * d:4 * d]
        yield
        _store_slabs(h2_ref, h2, r0 * SLAB)
        yield
        h_hi = h2.astype(BF16)
        r1 = h2 - h_hi.astype(F32)
        h_mid = r1.astype(BF16)
        h_lo = (r1 - h_mid.astype(F32)).astype(BF16)
        yield
        p_hi = _dot(h_hi, rw_ref[...])
        yield
        p_mid = _dot(h_mid, rw_ref[:, :2 * LANES])
        yield
        p_lo = _dot(h_lo, rw_ref[:, :LANES])
        yield
        small = (p_lo + p_mid[:, LANES:]) + p_hi[:, 2 * LANES:]
        vals = ((small + (p_mid[:, :LANES] + p_hi[:, LANES:2 * LANES])) + p_hi[:, :LANES]) + rb_ref[...]
        yield
        lane = lax.broadcasted_iota(jnp.int32, vals.shape, 1)
        lane_f = lane.astype(F32)
        idx_out = jnp.zeros(vals.shape, F32)
        top = []
        for k in range(TOP_K):
            mx = jnp.max(vals, axis=-1, keepdims=True)
            yield
            ix = jnp.min(jnp.where(vals == mx, lane_f, float(LANES)), axis=-1, keepdims=True)
            yield
            top.append(mx)
            idx_out = jnp.where(lane == k, ix, idx_out)
            vals = jnp.where(lane_f == ix, -jnp.inf, vals)
            yield
        ex = [jnp.exp(t - top[0]) for t in top]
        tot = ex[0] + ex[1] + ex[2] + ex[3]
        yield
        gate_out = jnp.zeros(vals.shape, F32)
        for k in range(TOP_K):
            gate_out = jnp.where(lane == k, ex[k] / tot, gate_out)
        idx_ref[rs, :] = idx_out.astype(jnp.int32)
        gate_ref[rs, :] = gate_out
        yield

    _interleave([row_group(r0) for r0 in range(0, OUT_TILES * TILE, OUT_ROWS)])


def _out_call(x, mlf, mlb, hgf, hgb, o, go, mod_l, gml, ghg, wout, g2, rw, rb, tpb, n_batch):
    n, d = x.shape
    w = mlf.shape[1]
    row = lambda i: (i, 0)
    fix = lambda i: (0, 0)
    rw_p = jnp.zeros((d, LANES), F32).at[:, :N_EXPERTS].set(rw)
    rb_p = jnp.full((1, LANES), NEG, F32).at[0, :N_EXPERTS].set(rb)
    rows = OUT_TILES * TILE
    assert n % rows == 0
    return pl.pallas_call(
        functools.partial(_out_kernel, tpb=tpb, n_batch=n_batch),
        grid=(n // rows,),
        in_specs=[pl.BlockSpec((rows, d), row)] + [pl.BlockSpec((rows, w), row)] * 6 + [
            pl.BlockSpec((8, 6 * d), fix), pl.BlockSpec((1, w), fix), pl.BlockSpec((1, w), fix),
            pl.BlockSpec((d, d), fix), pl.BlockSpec((1, d), fix), pl.BlockSpec((d, LANES), fix),
            pl.BlockSpec((1, LANES), fix)],
        out_specs=[pl.BlockSpec((rows, d), row), pl.BlockSpec((rows * SLAB, LANES), row),
                   pl.BlockSpec((rows, LANES), row), pl.BlockSpec((rows, LANES), row)],
        out_shape=[jax.ShapeDtypeStruct((n, d), F32), jax.ShapeDtypeStruct((n * SLAB, LANES), F32),
                   jax.ShapeDtypeStruct((n, LANES), jnp.int32), jax.ShapeDtypeStruct((n, LANES), F32)],
        scratch_shapes=[pltpu.VMEM((d, 3 * LANES), BF16)],
        compiler_params=_cparams(1),
        name="outproj",
    )(x, mlf, mlb, hgf, hgb, o, go, mod_l, gml.reshape(1, w), ghg.reshape(1, w), wout, g2.reshape(1, d), rw_p, rb_p)


def _onehots(idx):
    lane = lax.broadcasted_iota(jnp.int32, idx.shape, 1)
    return [lane == idx[:, k:k + 1] for k in range(TOP_K)]


def _rank_kernel(idx_ref, rank_ref, cnt_ref, carry_ref):
    @pl.when(pl.program_id(0) == 0)
    def _():
        carry_ref[...] = jnp.zeros_like(carry_ref)

    hots = _onehots(idx_ref[...])
    m = jnp.zeros(idx_ref.shape, F32)
    for hk in hots:
        m = m + jnp.where(hk, 1.0, 0.0)
    t_i = lax.broadcasted_iota(jnp.int32, (TILE, TILE), 0)
    s_i = lax.broadcasted_iota(jnp.int32, (TILE, TILE), 1)
    before = _dot((s_i < t_i).astype(BF16), m.astype(BF16)) + carry_ref[...]
    lane = lax.broadcasted_iota(jnp.int32, idx_ref.shape, 1)
    out = jnp.zeros(idx_ref.shape, F32)
    for k, hk in enumerate(hots):
        rk = jnp.sum(jnp.where(hk, before, 0.0), axis=-1, keepdims=True)
        out = jnp.where(lane == k, rk, out)
    rank_ref[...] = out.astype(jnp.int32)
    carry_ref[...] = carry_ref[...] + jnp.sum(m, axis=0, keepdims=True)
    cnt_ref[...] = carry_ref[...]


def _rank_call(idx):
    n = idx.shape[0]
    return pl.pallas_call(
        _rank_kernel,
        grid=(n // TILE,),
        in_specs=[pl.BlockSpec((TILE, LANES), lambda i: (i, 0))],
        out_specs=[pl.BlockSpec((TILE, LANES), lambda i: (i, 0)), pl.BlockSpec((1, LANES), lambda i: (0, 0))],
        out_shape=[jax.ShapeDtypeStruct((n, LANES), jnp.int32), jax.ShapeDtypeStruct((1, LANES), F32)],
        scratch_shapes=[pltpu.VMEM((1, LANES), F32)],
        compiler_params=_cparams(1),
        name="rank",
    )(idx)


def _dest_kernel(idx_ref, rank_ref, cnt_ref, dest_ref, meta_ref, *, n_blocks):
    cnt = cnt_ref[...]
    padded = jnp.floor((cnt + (MOE_BLK - 1)) * (1.0 / MOE_BLK)) * MOE_BLK
    e_i = lax.broadcasted_iota(jnp.int32, (LANES, LANES), 0)
    e_j = lax.broadcasted_iota(jnp.int32, (LANES, LANES), 1)
    pad_start = _dot_hi(jnp.broadcast_to(padded, (8, LANES)), (e_i < e_j).astype(F32))[0:1]
    pad_end = pad_start + padded
    lane = lax.broadcasted_iota(jnp.int32, idx_ref.shape, 1)
    rank = rank_ref[...]
    out = jnp.zeros(idx_ref.shape, jnp.int32)
    for k, hk in enumerate(_onehots(idx_ref[...])):
        st = jnp.sum(jnp.where(hk, pad_start, 0.0), axis=-1, keepdims=True).astype(jnp.int32)
        out = jnp.where(lane == k, st + rank[:, k:k + 1], out)
    dest_ref[...] = out

    @pl.when(pl.program_id(0) == 0)
    def _():
        blk = lax.broadcasted_iota(jnp.int32, meta_ref.shape, 0).astype(F32) * MOE_BLK
        lane_m = lax.broadcasted_iota(jnp.int32, meta_ref.shape, 1)
        done = jnp.where(jnp.logical_and(lane_m < N_EXPERTS, pad_end <= blk), 1.0, 0.0)
        be = jnp.minimum(jnp.sum(done, axis=-1, keepdims=True), N_EXPERTS - 1.0)
        used = jnp.sum(jnp.where(lane_m[0:1] < N_EXPERTS, padded, 0.0), axis=-1, keepdims=True) * (1.0 / MOE_BLK)
        diag = lane_m == lax.broadcasted_iota(jnp.int32, meta_ref.shape, 0)
        end_col = jnp.sum(jnp.where(diag, pad_end, 0.0), axis=-1, keepdims=True)
        pad_col = jnp.sum(jnp.where(diag, padded, 0.0), axis=-1, keepdims=True)
        meta = jnp.where(lane_m == 0, be, jnp.where(lane_m == 1, used, jnp.where(lane_m == 2, end_col, pad_col)))
        meta_ref[...] = jnp.where(lane_m < 4, meta, 0.0).astype(jnp.int32)


def _dest_call(idx, rank, cnt, n_blocks):
    n = idx.shape[0]
    nb_pad = -(-n_blocks // 8) * 8
    row = lambda i: (i, 0)
    return pl.pallas_call(
        functools.partial(_dest_kernel, n_blocks=n_blocks),
        grid=(n // TILE,),
        in_specs=[pl.BlockSpec((TILE, LANES), row), pl.BlockSpec((TILE, LANES), row),
                  pl.BlockSpec((1, LANES), lambda i: (0, 0))],
        out_specs=[pl.BlockSpec((TILE, LANES), row), pl.BlockSpec((nb_pad, LANES), lambda i: (0, 0))],
        out_shape=[jax.ShapeDtypeStruct((n, LANES), jnp.int32), jax.ShapeDtypeStruct((nb_pad, LANES), jnp.int32)],
        compiler_params=_cparams(1),
        name="dest",
    )(idx, rank, cnt)


ROW_UNROLL = 8


def _scatter_kernel(dest_ref, zinfo_ref, h_ref, xb_ref, zbuf, sem):
    blk_rows = MOE_BLK * SLAB

    @pl.when(pl.program_id(0) == 0)
    def _():
        zbuf[...] = jnp.zeros_like(zbuf)
        n_blocks = xb_ref.shape[0] // blk_rows
        used = zinfo_ref[2 * N_EXPERTS]
        for stage in ("start", "wait"):
            for e in range(N_EXPERTS):
                @pl.when(zinfo_ref[2 * e + 1] > 0)
                def _():
                    first = pl.multiple_of((zinfo_ref[2 * e] - MOE_BLK) * SLAB, blk_rows)
                    cp = pltpu.make_async_copy(zbuf, xb_ref.at[pl.ds(first, blk_rows), :], sem)
                    cp.start() if stage == "start" else cp.wait()

                @pl.when(used + e < n_blocks)
                def _():
                    first = pl.multiple_of((used + e) * blk_rows, blk_rows)
                    cp = pltpu.make_async_copy(zbuf, xb_ref.at[pl.ds(first, blk_rows), :], sem)
                    cp.start() if stage == "start" else cp.wait()

    def issue(g, c):
        for u in range(ROW_UNROLL):
            src = h_ref.at[pl.ds(pl.multiple_of((g * ROW_UNROLL + u) * SLAB, SLAB), SLAB), :]
            for k in range(TOP_K):
                dst = pl.multiple_of(dest_ref[g * (ROW_UNROLL * TOP_K) + u * TOP_K + k] * SLAB, SLAB)
                pltpu.make_async_copy(src, xb_ref.at[pl.ds(dst, SLAB), :], sem).start(priority=k % 2)
        return c

    lax.fori_loop(0, TILE // ROW_UNROLL, issue, 0)
    all_rows = xb_ref.at[pl.ds(0, TILE * TOP_K * SLAB), :]
    pltpu.make_async_copy(all_rows, all_rows, sem).wait()


def _scatter_call(dest_flat, zinfo, h2s, n_rows):
    n = h2s.shape[0] // SLAB
    return pl.pallas_call(
        _scatter_kernel,
        grid=(n // TILE,),
        in_specs=[pl.BlockSpec((TILE * TOP_K,), lambda i: (i,), memory_space=pltpu.SMEM),
                  pl.BlockSpec(memory_space=pltpu.SMEM),
                  pl.BlockSpec((TILE * SLAB, LANES), lambda i: (i, 0))],
        out_specs=pl.BlockSpec(memory_space=pl.ANY),
        out_shape=jax.ShapeDtypeStruct((n_rows * SLAB, LANES), F32),
        scratch_shapes=[pltpu.VMEM((MOE_BLK * SLAB, LANES), F32), pltpu.SemaphoreType.DMA(())],
        compiler_params=_cparams(1),
        name="scatter",
    )(dest_flat, zinfo, h2s)


CAST_ROWS = 64


def _cast_rows(src_ref, dst_ref):
    def body(r, c):
        rows = pl.ds(pl.multiple_of(r * CAST_ROWS, CAST_ROWS), CAST_ROWS)
        dst_ref[rows, :] = src_ref[rows, :].astype(BF16)
        return c

    lax.fori_loop(0, src_ref.shape[0] // CAST_ROWS, body, 0)


def _expert_kernel(be_ref, used_ref, nblk_ref, x_ref, wgu_hbm, bgu_ref, wd_hbm, bd_ref, y_ref,
                   wgu_f, wd_f, wgu_bf, wd_bf, slot_ref, sem, *, layer):
    i = pl.program_id(0)
    used = used_ref[0]
    live = i < used
    e = be_ref[i]
    new_expert = jnp.logical_or(i == 0, e != be_ref[jnp.maximum(i - 1, 0)])

    def fetch(expert, slot):
        return (pltpu.make_async_copy(wgu_hbm.at[layer, expert], wgu_f.at[slot], sem.at[0, slot]),
                pltpu.make_async_copy(wd_hbm.at[layer, expert], wd_f.at[slot], sem.at[1, slot]))

    @pl.when(jnp.logical_and(live, i == 0))
    def _():
        slot_ref[0] = 0
        for cp in fetch(e, 0):
            cp.start()

    @pl.when(jnp.logical_not(live))
    def _():
        y_ref[...] = jnp.zeros_like(y_ref)

    @pl.when(jnp.logical_and(live, new_expert))
    def _():
        slot = slot_ref[0]
        for cp in fetch(e, slot):
            cp.wait()
        nxt = i + nblk_ref[e]

        @pl.when(nxt < used)
        def _():
            for cp in fetch(be_ref[nxt], 1 - slot):
                cp.start()

        _cast_rows(wgu_f.at[slot], wgu_bf)
        _cast_rows(wd_f.at[slot], wd_bf)
        slot_ref[0] = 1 - slot

    @pl.when(live)
    def _():
        de = wd_bf.shape[0]
        gu = _dot(_load_slabs(x_ref, MOE_BLK).astype(BF16), wgu_bf[...]) + bgu_ref[...]
        glu = jnp.minimum(gu[:, :de], SWIGLU_LIMIT)
        lin = jnp.clip(gu[:, de:], -SWIGLU_LIMIT, SWIGLU_LIMIT)
        act = glu * _sigmoid(SWIGLU_ALPHA * glu) * (lin + 1.0)
        _store_slabs(y_ref, _dot(act.astype(BF16), wd_bf[...]) + bd_ref[...])


def _expert_call(layer, be, used, nblk, xb, wgu, bgu, wd, bd):
    n_rows = xb.shape[0] // SLAB
    depth, e, d, de2 = wgu.shape
    nb = n_rows // MOE_BLK
    last_live = lambda i, used: jnp.maximum(jnp.minimum(i, used[0] - 1), 0)
    blk = lambda i, be, used, nblk: (last_live(i, used), 0)
    bsel = lambda i, be, used, nblk: (layer, be[last_live(i, used)], 0, 0)
    return pl.pallas_call(
        functools.partial(_expert_kernel, layer=layer),
        grid_spec=pltpu.PrefetchScalarGridSpec(
            num_scalar_prefetch=3,
            grid=(nb,),
            in_specs=[pl.BlockSpec((MOE_BLK * SLAB, LANES), blk),
                      pl.BlockSpec(memory_space=pl.ANY), pl.BlockSpec((None, None, 1, de2), bsel),
                      pl.BlockSpec(memory_space=pl.ANY), pl.BlockSpec((None, None, 1, d), bsel)],
            out_specs=pl.BlockSpec((MOE_BLK * SLAB, LANES), lambda i, be, used, nblk: (i, 0)),
            scratch_shapes=[pltpu.VMEM((2, d, de2), F32), pltpu.VMEM((2, de2 // 2, d), F32),
                            pltpu.VMEM((d, de2), BF16), pltpu.VMEM((de2 // 2, d), BF16),
                            pltpu.SMEM((1,), jnp.int32), pltpu.SemaphoreType.DMA((2, 2))],
        ),
        out_shape=jax.ShapeDtypeStruct((n_rows * SLAB, LANES), F32),
        compiler_params=_cparams(1, vmem=EXPERT_VMEM_LIMIT),
        name="expert",
    )(be, used, nblk, xb, wgu, bgu.reshape(depth, e, 1, de2), wd, bd.reshape(depth, e, 1, d))


def _combine_kernel(dest_ref, dest_next_ref, x_ref, gate_ref, mod_ref, fg_ref, yb_ref, o_ref, buf, sem,
                    *, tpb, n_batch, final):
    i = pl.program_id(0)
    slot = i % 2

    def gather(idx_ref, to_slot):
        def issue(g, c):
            for u in range(ROW_UNROLL):
                rows = pl.ds(pl.multiple_of((g * ROW_UNROLL + u) * SLAB, SLAB), SLAB)
                for k in range(TOP_K):
                    src = pl.multiple_of(idx_ref[g * (ROW_UNROLL * TOP_K) + u * TOP_K + k] * SLAB, SLAB)
                    pltpu.make_async_copy(yb_ref.at[pl.ds(src, SLAB), :], buf.at[to_slot, k, rows, :],
                                          sem.at[to_slot]).start(priority=k % 2)
            return c

        lax.fori_loop(0, TILE // ROW_UNROLL, issue, 0)

    @pl.when(i == 0)
    def _():
        gather(dest_ref, 0)

    @pl.when(i + 1 < pl.num_programs(0))
    def _():
        gather(dest_next_ref, 1 - slot)

    cur = buf.at[slot]
    pltpu.make_async_copy(cur, cur, sem.at[slot]).wait()
    d = x_ref.shape[1]
    mod = _mod_row(mod_ref, tpb, n_batch)
    gates = gate_ref[...]
    f = gates[:, 0:1] * _load_slabs(cur.at[0], TILE)
    for k in range(1, TOP_K):
        f = f + gates[:, k:k + 1] * _load_slabs(cur.at[k], TILE)
    x = x_ref[...] + mod[:, 5 * d:6 * d] * f
    o_ref[...] = _rms(x, fg_ref[...]) if final else x


def _combine_call(dest_flat, x, gates, mod_l, fg, yb, tpb, n_batch, final):
    n, d = x.shape
    if final:
        out_rows = n - n_batch * TILE
        omap = lambda i: ((i // tpb) * (tpb - 1) + jnp.maximum(i % tpb - 1, 0), 0)
    else:
        out_rows = n
        omap = lambda i: (i, 0)
    nt = n // TILE
    return pl.pallas_call(
        functools.partial(_combine_kernel, tpb=tpb, n_batch=n_batch, final=final),
        grid=(nt,),
        in_specs=[pl.BlockSpec((TILE * TOP_K,), lambda i: (i,), memory_space=pltpu.SMEM),
                  pl.BlockSpec((TILE * TOP_K,), lambda i: (jnp.minimum(i + 1, nt - 1),), memory_space=pltpu.SMEM),
                  pl.BlockSpec((TILE, d), lambda i: (i, 0)),
                  pl.BlockSpec((TILE, LANES), lambda i: (i, 0)),
                  pl.BlockSpec((8, 6 * d), lambda i: (0, 0)),
                  pl.BlockSpec((1, d), lambda i: (0, 0)),
                  pl.BlockSpec(memory_space=pl.ANY)],
        out_specs=pl.BlockSpec((TILE, d), omap),
        out_shape=jax.ShapeDtypeStruct((out_rows, d), F32),
        scratch_shapes=[pltpu.VMEM((2, TOP_K, TILE * SLAB, LANES), F32), pltpu.SemaphoreType.DMA((2,))],
        compiler_params=_cparams(1),
        name="combine",
    )(dest_flat, dest_flat, x, gates, mod_l, fg.reshape(1, d), yb)


def kernel(x, c, ctx, c_ctx, w_ada, b_ada, norm1_g, w_in, mlstm_conv, mlstm_gate_b, mlstm_norm_g, hgrn_conv, hgrn_f_b,
           hgrn_lb_raw, hgrn_norm_g, w_out, norm2_g, router_w, router_b, w_gu, b_gu, w_down, b_down, final_g):
    n_batch, seq, d = x.shape
    ctx_len = ctx.shape[1]
    depth = w_ada.shape[0]
    assert ctx_len == TILE and seq % TILE == 0 and n_batch + 1 <= 8 and d == SLAB * LANES
    tpb = (ctx_len + seq) // TILE
    nct = ctx_len // TILE
    n = n_batch * (ctx_len + seq)
    n_blocks = -(-(n * TOP_K) // MOE_BLK) + N_EXPERTS
    n_rows = n_blocks * MOE_BLK

    xa = jnp.concatenate([ctx, x], axis=1).reshape(n, d)
    cond = jnp.zeros((8, d), F32).at[:n_batch].set(c).at[n_batch].set(c_ctx)
    mod = _ada_call(cond, w_ada, b_ada)

    lb_w = jax.nn.softmax(hgrn_lb_raw.astype(F32), axis=0)
    lower = jnp.cumsum(lb_w, axis=0) - lb_w[0]

    n_gate = 4 * ML_HEADS
    g0 = 3 * 512
    w_in_p = jnp.concatenate([w_in[:, :, :g0], jnp.pad(w_in[:, :, g0:g0 + n_gate], ((0, 0), (0, 0), (0, LANES - n_gate))),
                              w_in[:, :, g0 + n_gate:]], axis=2).astype(BF16)
    w_out_b = w_out.astype(BF16)

    for l in range(depth):
        last = l == depth - 1
        q, k, hq, hv, v, o, gates, ff, fb, go = _inconv_call(xa, mod[l], norm1_g[l], w_in_p[l], mlstm_conv[l],
                                                             hgrn_conv[l], tpb, n_batch)
        mlf, mlb, hgf, hgb = _mixers_call(q, k, v, gates, mlstm_gate_b[l], hq, hv, ff, fb, hgrn_f_b[l], lower[l],
                                          n_batch, nct)
        xa, h2, idx, gate = _out_call(xa, mlf, mlb, hgf, hgb, o, go, mod[l], mlstm_norm_g[l], hgrn_norm_g[l], w_out_b[l],
                                      norm2_g[l], router_w[l], router_b[l], tpb, n_batch)
        rank, cnt = _rank_call(idx)
        dest, meta = _dest_call(idx, rank, cnt, n_blocks)
        dest_flat = dest[:, :TOP_K].reshape(-1)
        zinfo = jnp.concatenate([meta[:N_EXPERTS, 2:4].reshape(-1), meta[0:1, 1]])
        xb = _scatter_call(dest_flat, zinfo, h2, n_rows)
        nblk = meta[:N_EXPERTS, 3] // MOE_BLK
        yb = _expert_call(l, meta[:n_blocks, 0], meta[0:1, 1], nblk, xb, w_gu, b_gu, w_down, b_down)
        xa = _combine_call(dest_flat, xa, gate, mod[l], final_g, yb, tpb, n_batch, last)
    return xa.reshape(n_batch, seq, d)
```

```python
import functools

import jax
import jax.numpy as jnp
from jax import lax
from jax.experimental import pallas as pl
from jax.experimental.pallas import tpu as pltpu

F32 = jnp.float32
BF16 = jnp.bfloat16
HIGHEST = lax.Precision.HIGHEST

GRID_W = 64
CHUNK = 64
ML_HEADS = 4
ML_QK = 64
ML_V = 128
HG_HEADS = 4
HG_DK = 128
N_EXPERTS = 32
TOP_K = 4
SWIGLU_LIMIT = 7.0
SWIGLU_ALPHA = 1.702
EPS = 1e-6

TILE = 256
OUT_TILES = 2
OUT_ROWS = 128
MOE_BLK = 256
LANES = 128
SUB = 16
NEG = -1e30
VMEM_LIMIT = 48 * 1024 * 1024
EXPERT_VMEM_LIMIT = 56 * 1024 * 1024


def _cparams(n_axes, vmem=VMEM_LIMIT):
    return pltpu.CompilerParams(dimension_semantics=("arbitrary",) * n_axes, vmem_limit_bytes=vmem)


def _dot(a, b):
    return jnp.dot(a, b, preferred_element_type=F32)


def _dot_nt(a, b):
    return lax.dot_general(a, b, (((1,), (1,)), ((), ())), preferred_element_type=F32)


def _dot_tn(a, b):
    return lax.dot_general(a, b, (((0,), (0,)), ((), ())), preferred_element_type=F32)


def _dot_hi(a, b):
    return jnp.dot(a, b, precision=HIGHEST, preferred_element_type=F32)


def _rms(x, g):
    return x * lax.rsqrt(jnp.mean(x * x, axis=-1, keepdims=True) + EPS) * g


def _sigmoid(x):
    return 1.0 / (1.0 + jnp.exp(-x))


def _log_sigmoid(x):
    return jnp.minimum(x, 0.0) - jnp.log(1.0 + jnp.exp(-jnp.abs(x)))


SLAB = 8


def _store_slabs(ref, x, base=0):
    rows = x.shape[0]
    for s in range(SLAB):
        ref[pl.ds(base + s, rows, stride=SLAB), :] = x[:, s * LANES:(s + 1) * LANES]


def _load_slabs(ref, rows):
    return jnp.concatenate([ref[pl.ds(s, rows, stride=SLAB), :] for s in range(SLAB)], axis=1)


def _ada_kernel(cond_ref, w_ref, b_ref, o_ref):
    c = cond_ref[...]
    o_ref[...] = _dot_hi(c * _sigmoid(c), w_ref[...]) + b_ref[...]


def _ada_call(cond, w_ada, b_ada):
    depth, d, n6 = w_ada.shape
    tn = 1024
    return pl.pallas_call(
        _ada_kernel,
        grid=(depth, n6 // tn),
        in_specs=[
            pl.BlockSpec((8, d), lambda l, j: (0, 0)),
            pl.BlockSpec((None, d, tn), lambda l, j: (l, 0, j)),
            pl.BlockSpec((None, 1, tn), lambda l, j: (l, 0, j)),
        ],
        out_specs=pl.BlockSpec((None, 8, tn), lambda l, j: (l, 0, j)),
        out_shape=jax.ShapeDtypeStruct((depth, 8, n6), F32),
        compiler_params=_cparams(2),
        name="ada",
    )(cond, w_ada, b_ada.reshape(depth, 1, n6))


def _mod_row(mod_ref, tpb, n_batch, tile=None):
    i = pl.program_id(0) if tile is None else tile
    row = jnp.where(i % tpb == 0, n_batch, i // tpb)
    return mod_ref[pl.ds(row, 1), :]


IN_SEGS = (512, 512, 512, 128, 1024, 512, 512, 512)


def _inproj_kernel(x_ref, mod_ref, g_ref, w_ref, *outs, tpb, n_batch):
    d = x_ref.shape[1]
    mod = _mod_row(mod_ref, tpb, n_batch)
    h = _rms(x_ref[...], g_ref[...]) * (1.0 + mod[:, d:2 * d]) + mod[:, 0:d]
    hb = h.astype(BF16)
    off = 0
    for o_ref, n in zip(outs, IN_SEGS):
        o_ref[...] = _dot(hb, w_ref[:, off:off + n])
        off += n


def _inproj_call(x, mod_l, g, w_p, tpb, n_batch):
    n, d = x.shape
    nw = w_p.shape[1]
    return pl.pallas_call(
        functools.partial(_inproj_kernel, tpb=tpb, n_batch=n_batch),
        grid=(n // TILE,),
        in_specs=[
            pl.BlockSpec((TILE, d), lambda i: (i, 0)),
            pl.BlockSpec((8, 6 * d), lambda i: (0, 0)),
            pl.BlockSpec((1, d), lambda i: (0, 0)),
            pl.BlockSpec((d, nw), lambda i: (0, 0)),
        ],
        out_specs=[pl.BlockSpec((TILE, s), lambda i: (i, 0)) for s in IN_SEGS],
        out_shape=[jax.ShapeDtypeStruct((n, s), F32) for s in IN_SEGS],
        compiler_params=_cparams(1),
        name="inproj",
    )(x, mod_l, g.reshape(1, d), w_p)


def _conv_kernel(qk_m, qk_p, qk_n, qi_m, qi_p, qi_n, wq_ref, wi_ref, q_ref, k_ref, hq_ref, hv_ref, *, tpb):
    n_ext = TILE + 2 * GRID_W
    j = pl.program_id(0) % tpb
    xr = lax.broadcasted_iota(jnp.int32, (n_ext, 1), 0)
    jv = jnp.zeros((n_ext, 1), jnp.int32) + j
    is_ctx = jv == 0
    col = xr % GRID_W
    ok_l = jnp.logical_or(col != 0, jnp.logical_and(is_ctx, xr != GRID_W))
    ok_r = jnp.logical_or(col != GRID_W - 1, jnp.logical_and(is_ctx, xr != GRID_W + TILE - 1))
    hv = jnp.zeros((GRID_W, 1), jnp.int32) + j
    prev_ok = jnp.logical_and(hv != 0, hv != 1)
    next_ok = jnp.logical_and(hv != 0, hv != tpb - 1)
    vert = jnp.where(jnp.zeros((1, 1), jnp.int32) + j == 0, 0.0, 1.0)

    def conv(main, prev, nxt, w_ref, c0, c1):
        ext = jnp.concatenate([jnp.where(prev_ok, prev[:, c0:c1], 0.0), main[:, c0:c1],
                               jnp.where(next_ok, nxt[:, c0:c1], 0.0)], axis=0)
        shifted = (jnp.where(ok_l, pltpu.roll(ext, 1, 0), 0.0), ext,
                   jnp.where(ok_r, pltpu.roll(ext, n_ext - 1, 0), 0.0))
        acc = jnp.zeros((TILE, c1 - c0), F32)
        for kh in range(3):
            for kw in range(3):
                w_tap = w_ref[kh * 3 + kw:kh * 3 + kw + 1, c0:c1]
                acc = acc + shifted[kw][kh * GRID_W:kh * GRID_W + TILE] * (w_tap if kh == 1 else w_tap * vert)
        return acc * _sigmoid(acc)

    nq = ML_HEADS * ML_QK
    q_ref[...] = conv(qk_m, qk_p, qk_n, wq_ref, 0, nq) * (ML_QK ** -0.5)
    k_ref[...] = conv(qk_m, qk_p, qk_n, wq_ref, nq, 2 * nq)
    nh = HG_HEADS * HG_DK
    hq_ref[...] = conv(qi_m, qi_p, qi_n, wi_ref, 0, nh)
    hv_ref[...] = conv(qi_m, qi_p, qi_n, wi_ref, nh, 2 * nh)


def _conv_call(qk, qi, wq, wi, tpb):
    n = qk.shape[0]
    rpt = TILE // GRID_W
    nrow = n // GRID_W
    cq, ci = qk.shape[1], qi.shape[1]

    def main(i):
        return (i, 0)

    def prev(i):
        return (jnp.maximum(i * rpt - 1, 0), 0)

    def nxt(i):
        return (jnp.minimum((i + 1) * rpt, nrow - 1), 0)

    return pl.pallas_call(
        functools.partial(_conv_kernel, tpb=tpb),
        grid=(n // TILE,),
        in_specs=[
            pl.BlockSpec((TILE, cq), main), pl.BlockSpec((GRID_W, cq), prev), pl.BlockSpec((GRID_W, cq), nxt),
            pl.BlockSpec((TILE, ci), main), pl.BlockSpec((GRID_W, ci), prev), pl.BlockSpec((GRID_W, ci), nxt),
            pl.BlockSpec((9, cq), lambda i: (0, 0)),
            pl.BlockSpec((9, ci), lambda i: (0, 0)),
        ],
        out_specs=[pl.BlockSpec((TILE, s), main) for s in (cq // 2, cq // 2, ci // 2, ci // 2)],
        out_shape=[jax.ShapeDtypeStruct((n, s), F32) for s in (cq // 2, cq // 2, ci // 2, ci // 2)],
        compiler_params=_cparams(1),
        name="conv",
    )(qk, qk, qk, qi, qi, qi, wq.reshape(9, cq), wi.reshape(9, ci))


W_CHUNKS = TILE // CHUNK


def _tile_maps(nct, nt):
    def fwd(b, j):
        return (b * nt + j, 0)

    def bwd(b, j):
        return (b * nt + jnp.where(j < nct, nct - 1 - j, nt + nct - 1 - j), 0)

    return fwd, bwd


def _tri(fwd):
    t = lax.broadcasted_iota(jnp.int32, (CHUNK, CHUNK), 0)
    s = lax.broadcasted_iota(jnp.int32, (CHUNK, CHUNK), 1)
    return (s <= t) if fwd else (s >= t)


def _interleave(gens):
    live = list(gens)
    while live:
        for g in list(live):
            try:
                next(g)
            except StopIteration:
                live.remove(g)


def _mask_dot(mask_bf, x):
    a = x.astype(BF16)
    r = x - a.astype(F32)
    b = r.astype(BF16)
    c = (r - b.astype(F32)).astype(BF16)
    return (_dot(mask_bf, a) + _dot(mask_bf, b)) + _dot(mask_bf, c)


def _mlstm_stages(qf, kf, vf, gf, qb, kb, vb, gb, bias_ref, hf_ref, hb_ref, ct_ref, m_ref):
    assert CHUNK == ML_QK and 2 * ML_QK == LANES
    lane = lax.broadcasted_iota(jnp.int32, (CHUNK, LANES), 1)
    t_id = lax.broadcasted_iota(jnp.int32, (CHUNK, LANES), 0)
    lo = lane < ML_QK
    lo_row = lo[0:1, :]
    s_id = lane % ML_QK
    r128 = lax.broadcasted_iota(jnp.int32, (LANES, LANES), 0)
    c128 = lax.broadcasted_iota(jnp.int32, (LANES, LANES), 1)
    same_head = (r128 // ML_QK) == (c128 // ML_QK)
    ones_blk = jnp.ones((CHUNK, LANES), BF16)
    zeros_va = jnp.zeros((CHUNK, ML_V + LANES), BF16)

    def rep(arr, col):
        return jnp.broadcast_to(arr[:, col:col + 1], (CHUNK, LANES))

    def half_max(x, first):
        sel = lo if x.shape[0] == CHUNK else lo_row
        y = jnp.where(sel if first else jnp.logical_not(sel), x, -jnp.inf)
        return jnp.broadcast_to(jnp.max(y, axis=-1, keepdims=True), x.shape)

    dirs = ((qf, kf, vf, gf, hf_ref), (qb, kb, vb, gb, hb_ref))
    masks = [(jnp.where(_tri(fwd), 1.0, 0.0).astype(BF16), (s_id <= t_id) if fwd else (s_id >= t_id))
             for fwd in (True, False)]
    last_row = (CHUNK - 1, 0)
    orders = (tuple(range(W_CHUNKS)), tuple(range(W_CHUNKS - 1, -1, -1)))
    wv = ML_V + LANES
    shared, local = {}, {}

    def gates_of_chunk(d, c):
        rows = slice(c * CHUNK, (c + 1) * CHUNK)
        g_all = dirs[d][3][rows, :] + bias_ref[...]
        yield
        ls = _log_sigmoid(g_all)
        yield
        bcum = _mask_dot(masks[d][0], ls)
        yield
        g_t = g_all.T
        yield
        shared[(d, c)] = (g_all, bcum, g_t, bcum.T)
        yield

    def chunk_local(d, c, p):
        q_ref, k_ref, v_ref = dirs[d][:3]
        rows = slice(c * CHUNK, (c + 1) * CHUNK)
        g_all, bcum, g_t, b_t = shared[(d, c)]
        mask_p, last = masks[d][1], last_row[d]
        ci = [d * ML_HEADS + 2 * p + e for e in (0, 1)]
        cf = [(2 + d) * ML_HEADS + 2 * p + e for e in (0, 1)]
        b_rep = [rep(bcum, cf[e]) for e in (0, 1)]
        yield
        bp = jnp.where(lo, b_rep[0], b_rep[1])
        ip = jnp.where(lo, rep(g_all, ci[0]), rep(g_all, ci[1]))
        yield
        br = jnp.concatenate([b_t[cf[0]:cf[0] + 1, :], b_t[cf[1]:cf[1] + 1, :]], axis=1)
        ir = jnp.concatenate([g_t[ci[0]:ci[0] + 1, :], g_t[ci[1]:ci[1] + 1, :]], axis=1)
        yield
        dmat = jnp.where(mask_p, bp - br + ir, -jnp.inf)
        yield
        m_rep = [half_max(dmat, True), half_max(dmat, False)]
        yield
        q01 = q_ref[rows, p * LANES:(p + 1) * LANES].astype(BF16)
        k01 = k_ref[rows, p * LANES:(p + 1) * LANES]
        k01b = k01.astype(BF16)
        k_bd = jnp.where(same_head, jnp.concatenate([k01b, k01b], axis=0), jnp.zeros((), BF16))
        yield
        qk = _dot_nt(q01, k_bd)
        yield
        s = qk * jnp.exp(dmat - jnp.where(lo, m_rep[0], m_rep[1]))
        yield
        va = [jnp.concatenate([v_ref[rows, (2 * p + e) * ML_V:(2 * p + e + 1) * ML_V].astype(BF16), ones_blk], axis=1)
              for e in (0, 1)]
        v_bd = jnp.concatenate([jnp.concatenate([va[0], zeros_va], axis=1),
                                jnp.concatenate([zeros_va, va[1]], axis=1)], axis=0)
        yield
        nd_loc = _dot(s.astype(BF16), v_bd)
        yield
        gp = bp[last:last + 1, :]
        e_row = gp - br + ir
        me_rep = [half_max(e_row, True), half_max(e_row, False)]
        yield
        kw = (k01 * jnp.exp(gp - bp + ip - jnp.where(lo_row, me_rep[0], me_rep[1]))).astype(BF16)
        yield
        upd = _dot_tn(kw, jnp.concatenate(va, axis=1))
        yield
        g_tot = [b_rep[e][last:last + 1, :] for e in (0, 1)]
        local[(d, c, p)] = (q01, nd_loc, upd, b_rep, m_rep, g_tot, me_rep)
        yield

    def recurrence(d, p, e):
        h_ref = dirs[d][4]
        h = 2 * p + e
        ci = d * ML_HEADS + h
        m = m_ref[ci:ci + 1, :]
        ct = ct_ref[ci]
        zeros_ct = jnp.zeros((ML_QK, wv), BF16)
        yield
        for c in orders[d]:
            q01, nd_loc, upd, b_rep, m_rep, g_tot, me_rep = local[(d, c, p)]
            ct_b = ct.astype(BF16)
            qc = _dot(q01, jnp.concatenate([ct_b, zeros_ct] if e == 0 else [zeros_ct, ct_b], axis=0))
            yield
            a_inter = b_rep[e] + m
            m_t = jnp.maximum(a_inter, m_rep[e])
            yield
            w_loc = jnp.exp(m_rep[e] - m_t)
            w_int = jnp.exp(a_inter - m_t)
            yield
            num = w_loc * nd_loc[:, e * wv:e * wv + ML_V] + w_int * qc[:, 0:ML_V]
            den = w_loc * nd_loc[:, e * wv + ML_V:(e + 1) * wv] + w_int * qc[:, ML_V:wv]
            yield
            h_ref[c * CHUNK:(c + 1) * CHUNK, h * ML_V:(h + 1) * ML_V] = num / jnp.maximum(jnp.abs(den), jnp.exp(-m_t))
            yield
            m_new = jnp.maximum(g_tot[e] + m, me_rep[e])
            w_c = jnp.exp(g_tot[e] + m - m_new)
            w_u = jnp.exp(me_rep[e] - m_new)
            yield
            ct = (jnp.concatenate([w_c, w_c], axis=1) * ct
                  + jnp.concatenate([w_u, w_u], axis=1) * upd[e * ML_QK:(e + 1) * ML_QK, e * wv:(e + 1) * wv])
            m = m_new
            yield
        ct_ref[ci] = ct
        m_ref[ci:ci + 1, :] = m
        yield

    pairs = range(ML_HEADS // 2)
    return ([gates_of_chunk(d, c) for d in (0, 1) for c in orders[d]],
            [chunk_local(d, c, p) for c in range(W_CHUNKS) for d in (0, 1) for p in pairs],
            [recurrence(d, p, e) for d in (0, 1) for p in pairs for e in (0, 1)])


def _hgrn_stages(qf, vf, ff, qb, vb, fb, fbias_ref, lb_ref, of_ref, ob_ref, st_ref):
    n_sub = CHUNK // SUB
    rows = lax.broadcasted_iota(jnp.int32, (CHUNK, 1), 0)
    t_i = lax.broadcasted_iota(jnp.int32, (CHUNK, CHUNK), 0)
    s_i = lax.broadcasted_iota(jnp.int32, (CHUNK, CHUNK), 1)
    same_sub = (t_i // SUB) == (s_i // SUB)
    lb = lb_ref[...]
    dirs = ((qf, vf, ff, of_ref), (qb, vb, fb, ob_ref))
    masks = [_tri(fwd) for fwd in (True, False)]
    masks_bf = [jnp.where(m, 1.0, 0.0).astype(BF16) for m in masks]
    diag_ok = [jnp.logical_and(same_sub, m) for m in masks]
    orders = (tuple(range(W_CHUNKS)), tuple(range(W_CHUNKS - 1, -1, -1)))
    local = {}

    def chunk_local(d, c):
        q_ref, v_ref, f_ref, _ = dirs[d]
        fwd = d == 0
        last = CHUNK - 1 if fwd else 0
        rs = slice(c * CHUNK, (c + 1) * CHUNK)
        f = lb + (1.0 - lb) * _sigmoid(f_ref[rs, :] + fbias_ref[d:d + 1, :])
        yield
        kk = 1.0 - f
        lf = jnp.log(f)
        yield
        b = _mask_dot(masks_bf[d], lf)
        yield
        q = q_ref[rs, :]
        q_parts, k_parts = [], []
        for p in range(1, n_sub):
            if fwd:
                ref_row, q_ok, k_ok = p * SUB - 1, (rows // SUB) == p, rows < p * SUB
            else:
                ref_row, q_ok, k_ok = p * SUB, (rows // SUB) == p - 1, rows >= p * SUB
            r_p = b[ref_row:ref_row + 1, :]
            q_parts.append(jnp.where(q_ok, q * jnp.exp(jnp.minimum(b - r_p, 0.0)), 0.0))
            yield
            k_parts.append(jnp.where(k_ok, kk * jnp.exp(jnp.minimum(r_p - b, 0.0)), 0.0))
            yield
        mid = SUB // 2 - 1 if fwd else SUB // 2
        c_m = b[mid:mid + 1, :]
        for i in range(1, n_sub):
            c_m = jnp.where(rows >= i * SUB, b[i * SUB + mid:i * SUB + mid + 1, :], c_m)
        yield
        q_d = q * jnp.exp(b - c_m)
        yield
        k_d = kk * jnp.exp(c_m - b)
        yield
        g_row = b[last:last + 1, :]
        q_in = (q * jnp.exp(b)).astype(BF16)
        yield
        k_out = (kk * jnp.exp(g_row - b)).astype(BF16)
        decay = jnp.exp(g_row)
        yield
        for h in range(HG_HEADS):
            sl = slice(h * HG_DK, (h + 1) * HG_DK)
            qc = jnp.concatenate([x[:, sl] for x in q_parts], axis=1).astype(BF16)
            kc = jnp.concatenate([x[:, sl] for x in k_parts], axis=1).astype(BF16)
            yield
            a_off = _dot_nt(qc, kc)
            yield
            a_diag = _dot_nt(q_d[:, sl].astype(BF16), k_d[:, sl].astype(BF16))
            yield
            a = (a_off + jnp.where(diag_ok[d], a_diag, 0.0)).astype(BF16)
            vh = v_ref[rs, sl].astype(BF16)
            yield
            o_loc = _dot(a, vh)
            yield
            local[(d, c, h)] = (o_loc, q_in[:, sl], decay[:, sl], _dot_tn(vh, k_out[:, sl]))
            yield

    def recurrence(d, h):
        o_ref = dirs[d][3]
        st = st_ref[d * HG_HEADS + h]
        yield
        for c in orders[d]:
            o_loc, q_in, decay, upd = local[(d, c, h)]
            inter = _dot_nt(q_in, st.astype(BF16))
            yield
            o_ref[c * CHUNK:(c + 1) * CHUNK, h * HG_DK:(h + 1) * HG_DK] = o_loc + inter
            st = st * decay + upd
            yield
        st_ref[d * HG_HEADS + h] = st
        yield

    return ([chunk_local(d, c) for c in range(W_CHUNKS) for d in (0, 1)],
            [recurrence(d, h) for d in (0, 1) for h in range(HG_HEADS)])


def _mixers_kernel(qf, kf, vf, gf, qb, kb, vb, gb, bias_ref, hqf, hvf, hff, hqb, hvb, hfb, fbias_ref, lb_ref,
                   mlf_ref, mlb_ref, hgf_ref, hgb_ref, ct_ref, m_ref, st_ref):
    @pl.when(pl.program_id(1) == 0)
    def _():
        ct_ref[...] = jnp.zeros_like(ct_ref)
        m_ref[...] = jnp.zeros_like(m_ref)
        st_ref[...] = jnp.zeros_like(st_ref)

    ml_gates, ml_local, ml_rec = _mlstm_stages(qf, kf, vf, gf, qb, kb, vb, gb, bias_ref, mlf_ref, mlb_ref, ct_ref, m_ref)
    hg_local, hg_rec = _hgrn_stages(hqf, hvf, hff, hqb, hvb, hfb, fbias_ref, lb_ref, hgf_ref, hgb_ref, st_ref)
    _interleave(ml_gates)
    _interleave(ml_local + hg_local)
    _interleave(ml_rec + hg_rec)


def _mixers_call(q, k, v, gates, gate_b, hq, hv, ff, fb, f_b, lb, n_batch, nct):
    n = q.shape[0]
    nt = n // TILE // n_batch
    fwd, bwd = _tile_maps(nct, nt)
    dq, dv, w = q.shape[1], v.shape[1], hq.shape[1]
    fix = lambda b, j: (0, 0)
    ml_specs = lambda m: [pl.BlockSpec((TILE, dq), m), pl.BlockSpec((TILE, dq), m),
                          pl.BlockSpec((TILE, dv), m), pl.BlockSpec((TILE, LANES), m)]
    hg_specs = lambda m: [pl.BlockSpec((TILE, w), m)] * 3
    bias = jnp.zeros((1, LANES), F32).at[0, :4 * ML_HEADS].set(gate_b.reshape(-1))
    return pl.pallas_call(
        _mixers_kernel,
        grid=(n_batch, nt),
        in_specs=(ml_specs(fwd) + ml_specs(bwd) + [pl.BlockSpec((1, LANES), fix)]
                  + hg_specs(fwd) + hg_specs(bwd) + [pl.BlockSpec((2, w), fix), pl.BlockSpec((1, w), fix)]),
        out_specs=[pl.BlockSpec((TILE, dv), fwd), pl.BlockSpec((TILE, dv), bwd),
                   pl.BlockSpec((TILE, w), fwd), pl.BlockSpec((TILE, w), bwd)],
        out_shape=[jax.ShapeDtypeStruct((n, dv), F32)] * 2 + [jax.ShapeDtypeStruct((n, w), F32)] * 2,
        scratch_shapes=[pltpu.VMEM((2 * ML_HEADS, ML_QK, ML_V + LANES), F32), pltpu.VMEM((2 * ML_HEADS, LANES), F32),
                        pltpu.VMEM((2 * HG_HEADS, HG_DK, HG_DK), F32)],
        compiler_params=_cparams(2),
        name="mixers",
    )(q, k, v, gates, q, k, v, gates, bias, hq, hv, ff, hq, hv, fb, f_b, lb.reshape(1, w))


def _head_rms(h, g, width):
    parts = []
    for i in range(h.shape[1] // width):
        p = h[:, i * width:(i + 1) * width]
        parts.append(p * lax.rsqrt(jnp.mean(p * p, axis=-1, keepdims=True) + EPS))
    return jnp.concatenate(parts, axis=1) * g


def _out_kernel(x_ref, mlf, mlb, hgf, hgb, o_ref, go_ref, mod_ref, gml_ref, ghg_ref, wout_ref, g2_ref, rw_f32_ref, rb_ref,
                xo_ref, h2_ref, idx_ref, gate_ref, rw_ref, *, tpb, n_batch):
    d = x_ref.shape[1]

    @pl.when(pl.program_id(0) == 0)
    def _():
        w = rw_f32_ref[...]
        w_hi = w.astype(BF16)
        r1 = w - w_hi.astype(F32)
        w_mid = r1.astype(BF16)
        rw_ref[:, 0:LANES] = w_hi
        rw_ref[:, LANES:2 * LANES] = w_mid
        rw_ref[:, 2 * LANES:3 * LANES] = (r1 - w_mid.astype(F32)).astype(BF16)

    def row_group(r0):
        rs = slice(r0, r0 + OUT_ROWS)
        mod = _mod_row(mod_ref, tpb, n_batch, pl.program_id(0) * OUT_TILES + r0 // TILE)
        y_ml = _sigmoid(o_ref[rs, :]) * _head_rms(mlf[rs, :] + mlb[rs, :], gml_ref[...], ML_V)
        yield
        go = go_ref[rs, :]
        y_hg = go * _sigmoid(go) * _head_rms(hgf[rs, :] + hgb[rs, :], ghg_ref[...], HG_DK)
        yield
        y = jnp.concatenate([y_ml, y_hg], axis=1).astype(BF16)
        proj = _dot(y, wout_ref[...])
        yield
        x = x_ref[rs, :] + mod[:, 2 * d:3 * d] * proj
        xo_ref[rs, :] = x
        yield
        h2 = _rms(x, g2_ref[...]) * (1.0 + mod[:, 4 * d:5 * d]) + mod[:, 3 * d:4 * d]
        yield
        _store_slabs(h2_ref, h2, r0 * SLAB)
        yield
        h_hi = h2.astype(BF16)
        r1 = h2 - h_hi.astype(F32)
        h_mid = r1.astype(BF16)
        h_lo = (r1 - h_mid.astype(F32)).astype(BF16)
        yield
        p_hi = _dot(h_hi, rw_ref[...])
        yield
        p_mid = _dot(h_mid, rw_ref[:, :2 * LANES])
        yield
        p_lo = _dot(h_lo, rw_ref[:, :LANES])
        yield
        small = (p_lo + p_mid[:, LANES:]) + p_hi[:, 2 * LANES:]
        vals = ((small + (p_mid[:, :LANES] + p_hi[:, LANES:2 * LANES])) + p_hi[:, :LANES]) + rb_ref[...]
        yield
        lane = lax.broadcasted_iota(jnp.int32, vals.shape, 1)
        lane_f = lane.astype(F32)
        idx_out = jnp.zeros(vals.shape, F32)
        top = []
        for k in range(TOP_K):
            mx = jnp.max(vals, axis=-1, keepdims=True)
            yield
            ix = jnp.min(jnp.where(vals == mx, lane_f, float(LANES)), axis=-1, keepdims=True)
            yield
            top.append(mx)
            idx_out = jnp.where(lane == k, ix, idx_out)
            vals = jnp.where(lane_f == ix, -jnp.inf, vals)
            yield
        ex = [jnp.exp(t - top[0]) for t in top]
        tot = ex[0] + ex[1] + ex[2] + ex[3]
        yield
        gate_out = jnp.zeros(vals.shape, F32)
        for k in range(TOP_K):
            gate_out = jnp.where(lane == k, ex[k] / tot, gate_out)
        idx_ref[rs, :] = idx_out.astype(jnp.int32)
        gate_ref[rs, :] = gate_out
        yield

    _interleave([row_group(r0) for r0 in range(0, OUT_TILES * TILE, OUT_ROWS)])


def _out_call(x, mlf, mlb, hgf, hgb, o, go, mod_l, gml, ghg, wout, g2, rw, rb, tpb, n_batch):
    n, d = x.shape
    w = mlf.shape[1]
    row = lambda i: (i, 0)
    fix = lambda i: (0, 0)
    rw_p = jnp.zeros((d, LANES), F32).at[:, :N_EXPERTS].set(rw)
    rb_p = jnp.full((1, LANES), NEG, F32).at[0, :N_EXPERTS].set(rb)
    rows = OUT_TILES * TILE
    assert n % rows == 0
    return pl.pallas_call(
        functools.partial(_out_kernel, tpb=tpb, n_batch=n_batch),
        grid=(n // rows,),
        in_specs=[pl.BlockSpec((rows, d), row)] + [pl.BlockSpec((rows, w), row)] * 6 + [
            pl.BlockSpec((8, 6 * d), fix), pl.BlockSpec((1, w), fix), pl.BlockSpec((1, w), fix),
            pl.BlockSpec((d, d), fix), pl.BlockSpec((1, d), fix), pl.BlockSpec((d, LANES), fix),
            pl.BlockSpec((1, LANES), fix)],
        out_specs=[pl.BlockSpec((rows, d), row), pl.BlockSpec((rows * SLAB, LANES), row),
                   pl.BlockSpec((rows, LANES), row), pl.BlockSpec((rows, LANES), row)],
        out_shape=[jax.ShapeDtypeStruct((n, d), F32), jax.ShapeDtypeStruct((n * SLAB, LANES), F32),
                   jax.ShapeDtypeStruct((n, LANES), jnp.int32), jax.ShapeDtypeStruct((n, LANES), F32)],
        scratch_shapes=[pltpu.VMEM((d, 3 * LANES), BF16)],
        compiler_params=_cparams(1),
        name="outproj",
    )(x, mlf, mlb, hgf, hgb, o, go, mod_l, gml.reshape(1, w), ghg.reshape(1, w), wout, g2.reshape(1, d), rw_p, rb_p)


def _onehots(idx):
    lane = lax.broadcasted_iota(jnp.int32, idx.shape, 1)
    return [lane == idx[:, k:k + 1] for k in range(TOP_K)]


def _rank_kernel(idx_ref, rank_ref, cnt_ref, carry_ref):
    @pl.when(pl.program_id(0) == 0)
    def _():
        carry_ref[...] = jnp.zeros_like(carry_ref)

    hots = _onehots(idx_ref[...])
    m = jnp.zeros(idx_ref.shape, F32)
    for hk in hots:
        m = m + jnp.where(hk, 1.0, 0.0)
    t_i = lax.broadcasted_iota(jnp.int32, (TILE, TILE), 0)
    s_i = lax.broadcasted_iota(jnp.int32, (TILE, TILE), 1)
    before = _dot((s_i < t_i).astype(BF16), m.astype(BF16)) + carry_ref[...]
    lane = lax.broadcasted_iota(jnp.int32, idx_ref.shape, 1)
    out = jnp.zeros(idx_ref.shape, F32)
    for k, hk in enumerate(hots):
        rk = jnp.sum(jnp.where(hk, before, 0.0), axis=-1, keepdims=True)
        out = jnp.where(lane == k, rk, out)
    rank_ref[...] = out.astype(jnp.int32)
    carry_ref[...] = carry_ref[...] + jnp.sum(m, axis=0, keepdims=True)
    cnt_ref[...] = carry_ref[...]


def _rank_call(idx):
    n = idx.shape[0]
    return pl.pallas_call(
        _rank_kernel,
        grid=(n // TILE,),
        in_specs=[pl.BlockSpec((TILE, LANES), lambda i: (i, 0))],
        out_specs=[pl.BlockSpec((TILE, LANES), lambda i: (i, 0)), pl.BlockSpec((1, LANES), lambda i: (0, 0))],
        out_shape=[jax.ShapeDtypeStruct((n, LANES), jnp.int32), jax.ShapeDtypeStruct((1, LANES), F32)],
        scratch_shapes=[pltpu.VMEM((1, LANES), F32)],
        compiler_params=_cparams(1),
        name="rank",
    )(idx)


def _dest_kernel(idx_ref, rank_ref, cnt_ref, dest_ref, meta_ref, *, n_blocks):
    cnt = cnt_ref[...]
    padded = jnp.floor((cnt + (MOE_BLK - 1)) * (1.0 / MOE_BLK)) * MOE_BLK
    e_i = lax.broadcasted_iota(jnp.int32, (LANES, LANES), 0)
    e_j = lax.broadcasted_iota(jnp.int32, (LANES, LANES), 1)
    pad_start = _dot_hi(jnp.broadcast_to(padded, (8, LANES)), (e_i < e_j).astype(F32))[0:1]
    pad_end = pad_start + padded
    lane = lax.broadcasted_iota(jnp.int32, idx_ref.shape, 1)
    rank = rank_ref[...]
    out = jnp.zeros(idx_ref.shape, jnp.int32)
    for k, hk in enumerate(_onehots(idx_ref[...])):
        st = jnp.sum(jnp.where(hk, pad_start, 0.0), axis=-1, keepdims=True).astype(jnp.int32)
        out = jnp.where(lane == k, st + rank[:, k:k + 1], out)
    dest_ref[...] = out

    @pl.when(pl.program_id(0) == 0)
    def _():
        blk = lax.broadcasted_iota(jnp.int32, meta_ref.shape, 0).astype(F32) * MOE_BLK
        lane_m = lax.broadcasted_iota(jnp.int32, meta_ref.shape, 1)
        done = jnp.where(jnp.logical_and(lane_m < N_EXPERTS, pad_end <= blk), 1.0, 0.0)
        be = jnp.minimum(jnp.sum(done, axis=-1, keepdims=True), N_EXPERTS - 1.0)
        used = jnp.sum(jnp.where(lane_m[0:1] < N_EXPERTS, padded, 0.0), axis=-1, keepdims=True) * (1.0 / MOE_BLK)
        diag = lane_m == lax.broadcasted_iota(jnp.int32, meta_ref.shape, 0)
        end_col = jnp.sum(jnp.where(diag, pad_end, 0.0), axis=-1, keepdims=True)
        pad_col = jnp.sum(jnp.where(diag, padded, 0.0), axis=-1, keepdims=True)
        meta = jnp.where(lane_m == 0, be, jnp.where(lane_m == 1, used, jnp.where(lane_m == 2, end_col, pad_col)))
        meta_ref[...] = jnp.where(lane_m < 4, meta, 0.0).astype(jnp.int32)


def _dest_call(idx, rank, cnt, n_blocks):
    n = idx.shape[0]
    nb_pad = -(-n_blocks // 8) * 8
    row = lambda i: (i, 0)
    return pl.pallas_call(
        functools.partial(_dest_kernel, n_blocks=n_blocks),
        grid=(n // TILE,),
        in_specs=[pl.BlockSpec((TILE, LANES), row), pl.BlockSpec((TILE, LANES), row),
                  pl.BlockSpec((1, LANES), lambda i: (0, 0))],
        out_specs=[pl.BlockSpec((TILE, LANES), row), pl.BlockSpec((nb_pad, LANES), lambda i: (0, 0))],
        out_shape=[jax.ShapeDtypeStruct((n, LANES), jnp.int32), jax.ShapeDtypeStruct((nb_pad, LANES), jnp.int32)],
        compiler_params=_cparams(1),
        name="dest",
    )(idx, rank, cnt)


ROW_UNROLL = 8


def _scatter_kernel(dest_ref, zinfo_ref, h_ref, xb_ref, zbuf, sem):
    blk_rows = MOE_BLK * SLAB

    @pl.when(pl.program_id(0) == 0)
    def _():
        zbuf[...] = jnp.zeros_like(zbuf)
        n_blocks = xb_ref.shape[0] // blk_rows
        used = zinfo_ref[2 * N_EXPERTS]
        for stage in ("start", "wait"):
            for e in range(N_EXPERTS):
                @pl.when(zinfo_ref[2 * e + 1] > 0)
                def _():
                    first = pl.multiple_of((zinfo_ref[2 * e] - MOE_BLK) * SLAB, blk_rows)
                    cp = pltpu.make_async_copy(zbuf, xb_ref.at[pl.ds(first, blk_rows), :], sem)
                    cp.start() if stage == "start" else cp.wait()

                @pl.when(used + e < n_blocks)
                def _():
                    first = pl.multiple_of((used + e) * blk_rows, blk_rows)
                    cp = pltpu.make_async_copy(zbuf, xb_ref.at[pl.ds(first, blk_rows), :], sem)
                    cp.start() if stage == "start" else cp.wait()

    def issue(g, c):
        for u in range(ROW_UNROLL):
            src = h_ref.at[pl.ds(pl.multiple_of((g * ROW_UNROLL + u) * SLAB, SLAB), SLAB), :]
            for k in range(TOP_K):
                dst = pl.multiple_of(dest_ref[g * (ROW_UNROLL * TOP_K) + u * TOP_K + k] * SLAB, SLAB)
                pltpu.make_async_copy(src, xb_ref.at[pl.ds(dst, SLAB), :], sem).start(priority=k % 2)
        return c

    lax.fori_loop(0, TILE // ROW_UNROLL, issue, 0)
    all_rows = xb_ref.at[pl.ds(0, TILE * TOP_K * SLAB), :]
    pltpu.make_async_copy(all_rows, all_rows, sem).wait()


def _scatter_call(dest_flat, zinfo, h2s, n_rows):
    n = h2s.shape[0] // SLAB
    return pl.pallas_call(
        _scatter_kernel,
        grid=(n // TILE,),
        in_specs=[pl.BlockSpec((TILE * TOP_K,), lambda i: (i,), memory_space=pltpu.SMEM),
                  pl.BlockSpec(memory_space=pltpu.SMEM),
                  pl.BlockSpec((TILE * SLAB, LANES), lambda i: (i, 0))],
        out_specs=pl.BlockSpec(memory_space=pl.ANY),
        out_shape=jax.ShapeDtypeStruct((n_rows * SLAB, LANES), F32),
        scratch_shapes=[pltpu.VMEM((MOE_BLK * SLAB, LANES), F32), pltpu.SemaphoreType.DMA(())],
        compiler_params=_cparams(1),
        name="scatter",
    )(dest_flat, zinfo, h2s)


CAST_ROWS = 64


def _cast_rows(src_ref, dst_ref):
    def body(r, c):
        rows = pl.ds(pl.multiple_of(r * CAST_ROWS, CAST_ROWS), CAST_ROWS)
        dst_ref[rows, :] = src_ref[rows, :].astype(BF16)
        return c

    lax.fori_loop(0, src_ref.shape[0] // CAST_ROWS, body, 0)


def _expert_kernel(be_ref, used_ref, nblk_ref, x_ref, wgu_hbm, bgu_ref, wd_hbm, bd_ref, y_ref,
                   wgu_f, wd_f, wgu_bf, wd_bf, slot_ref, sem, *, layer):
    i = pl.program_id(0)
    used = used_ref[0]
    live = i < used
    e = be_ref[i]
    new_expert = jnp.logical_or(i == 0, e != be_ref[jnp.maximum(i - 1, 0)])

    def fetch(expert, slot):
        return (pltpu.make_async_copy(wgu_hbm.at[layer, expert], wgu_f.at[slot], sem.at[0, slot]),
                pltpu.make_async_copy(wd_hbm.at[layer, expert], wd_f.at[slot], sem.at[1, slot]))

    @pl.when(jnp.logical_and(live, i == 0))
    def _():
        slot_ref[0] = 0
        for cp in fetch(e, 0):
            cp.start()

    @pl.when(jnp.logical_not(live))
    def _():
        y_ref[...] = jnp.zeros_like(y_ref)

    @pl.when(jnp.logical_and(live, new_expert))
    def _():
        slot = slot_ref[0]
        for cp in fetch(e, slot):
            cp.wait()
        nxt = i + nblk_ref[e]

        @pl.when(nxt < used)
        def _():
            for cp in fetch(be_ref[nxt], 1 - slot):
                cp.start()

        _cast_rows(wgu_f.at[slot], wgu_bf)
        _cast_rows(wd_f.at[slot], wd_bf)
        slot_ref[0] = 1 - slot

    @pl.when(live)
    def _():
        de = wd_bf.shape[0]
        gu = _dot(_load_slabs(x_ref, MOE_BLK).astype(BF16), wgu_bf[...]) + bgu_ref[...]
        glu = jnp.minimum(gu[:, :de], SWIGLU_LIMIT)
        lin = jnp.clip(gu[:, de:], -SWIGLU_LIMIT, SWIGLU_LIMIT)
        act = glu * _sigmoid(SWIGLU_ALPHA * glu) * (lin + 1.0)
        _store_slabs(y_ref, _dot(act.astype(BF16), wd_bf[...]) + bd_ref[...])


def _expert_call(layer, be, used, nblk, xb, wgu, bgu, wd, bd):
    n_rows = xb.shape[0] // SLAB
    depth, e, d, de2 = wgu.shape
    nb = n_rows // MOE_BLK
    last_live = lambda i, used: jnp.maximum(jnp.minimum(i, used[0] - 1), 0)
    blk = lambda i, be, used, nblk: (last_live(i, used), 0)
    bsel = lambda i, be, used, nblk: (layer, be[last_live(i, used)], 0, 0)
    return pl.pallas_call(
        functools.partial(_expert_kernel, layer=layer),
        grid_spec=pltpu.PrefetchScalarGridSpec(
            num_scalar_prefetch=3,
            grid=(nb,),
            in_specs=[pl.BlockSpec((MOE_BLK * SLAB, LANES), blk),
                      pl.BlockSpec(memory_space=pl.ANY), pl.BlockSpec((None, None, 1, de2), bsel),
                      pl.BlockSpec(memory_space=pl.ANY), pl.BlockSpec((None, None, 1, d), bsel)],
            out_specs=pl.BlockSpec((MOE_BLK * SLAB, LANES), lambda i, be, used, nblk: (i, 0)),
            scratch_shapes=[pltpu.VMEM((2, d, de2), F32), pltpu.VMEM((2, de2 // 2, d), F32),
                            pltpu.VMEM((d, de2), BF16), pltpu.VMEM((de2 // 2, d), BF16),
                            pltpu.SMEM((1,), jnp.int32), pltpu.SemaphoreType.DMA((2, 2))],
        ),
        out_shape=jax.ShapeDtypeStruct((n_rows * SLAB, LANES), F32),
        compiler_params=_cparams(1, vmem=EXPERT_VMEM_LIMIT),
        name="expert",
    )(be, used, nblk, xb, wgu, bgu.reshape(depth, e, 1, de2), wd, bd.reshape(depth, e, 1, d))


def _combine_kernel(dest_ref, dest_next_ref, x_ref, gate_ref, mod_ref, fg_ref, yb_ref, o_ref, buf, sem,
                    *, tpb, n_batch, final):
    i = pl.program_id(0)
    slot = i % 2

    def gather(idx_ref, to_slot):
        def issue(g, c):
            for u in range(ROW_UNROLL):
                rows = pl.ds(pl.multiple_of((g * ROW_UNROLL + u) * SLAB, SLAB), SLAB)
                for k in range(TOP_K):
                    src = pl.multiple_of(idx_ref[g * (ROW_UNROLL * TOP_K) + u * TOP_K + k] * SLAB, SLAB)
                    pltpu.make_async_copy(yb_ref.at[pl.ds(src, SLAB), :], buf.at[to_slot, k, rows, :],
                                          sem.at[to_slot]).start(priority=k % 2)
            return c

        lax.fori_loop(0, TILE // ROW_UNROLL, issue, 0)

    @pl.when(i == 0)
    def _():
        gather(dest_ref, 0)

    @pl.when(i + 1 < pl.num_programs(0))
    def _():
        gather(dest_next_ref, 1 - slot)

    cur = buf.at[slot]
    pltpu.make_async_copy(cur, cur, sem.at[slot]).wait()
    d = x_ref.shape[1]
    mod = _mod_row(mod_ref, tpb, n_batch)
    gates = gate_ref[...]
    f = gates[:, 0:1] * _load_slabs(cur.at[0], TILE)
    for k in range(1, TOP_K):
        f = f + gates[:, k:k + 1] * _load_slabs(cur.at[k], TILE)
    x = x_ref[...] + mod[:, 5 * d:6 * d] * f
    o_ref[...] = _rms(x, fg_ref[...]) if final else x


def _combine_call(dest_flat, x, gates, mod_l, fg, yb, tpb, n_batch, final):
    n, d = x.shape
    if final:
        out_rows = n - n_batch * TILE
        omap = lambda i: ((i // tpb) * (tpb - 1) + jnp.maximum(i % tpb - 1, 0), 0)
    else:
        out_rows = n
        omap = lambda i: (i, 0)
    nt = n // TILE
    return pl.pallas_call(
        functools.partial(_combine_kernel, tpb=tpb, n_batch=n_batch, final=final),
        grid=(nt,),
        in_specs=[pl.BlockSpec((TILE * TOP_K,), lambda i: (i,), memory_space=pltpu.SMEM),
                  pl.BlockSpec((TILE * TOP_K,), lambda i: (jnp.minimum(i + 1, nt - 1),), memory_space=pltpu.SMEM),
                  pl.BlockSpec((TILE, d), lambda i: (i, 0)),
                  pl.BlockSpec((TILE, LANES), lambda i: (i, 0)),
                  pl.BlockSpec((8, 6 * d), lambda i: (0, 0)),
                  pl.BlockSpec((1, d), lambda i: (0, 0)),
                  pl.BlockSpec(memory_space=pl.ANY)],
        out_specs=pl.BlockSpec((TILE, d), omap),
        out_shape=jax.ShapeDtypeStruct((out_rows, d), F32),
        scratch_shapes=[pltpu.VMEM((2, TOP_K, TILE * SLAB, LANES), F32), pltpu.SemaphoreType.DMA((2,))],
        compiler_params=_cparams(1),
        name="combine",
    )(dest_flat, dest_flat, x, gates, mod_l, fg.reshape(1, d), yb)


def kernel(x, c, ctx, c_ctx, w_ada, b_ada, norm1_g, w_in, mlstm_conv, mlstm_gate_b, mlstm_norm_g, hgrn_conv, hgrn_f_b,
           hgrn_lb_raw, hgrn_norm_g, w_out, norm2_g, router_w, router_b, w_gu, b_gu, w_down, b_down, final_g):
    n_batch, seq, d = x.shape
    ctx_len = ctx.shape[1]
    depth = w_ada.shape[0]
    assert ctx_len == TILE and seq % TILE == 0 and n_batch + 1 <= 8 and d == SLAB * LANES
    tpb = (ctx_len + seq) // TILE
    nct = ctx_len // TILE
    n = n_batch * (ctx_len + seq)
    n_blocks = -(-(n * TOP_K) // MOE_BLK) + N_EXPERTS
    n_rows = n_blocks * MOE_BLK

    xa = jnp.concatenate([ctx, x], axis=1).reshape(n, d)
    cond = jnp.zeros((8, d), F32).at[:n_batch].set(c).at[n_batch].set(c_ctx)
    mod = _ada_call(cond, w_ada, b_ada)

    lb_w = jax.nn.softmax(hgrn_lb_raw.astype(F32), axis=0)
    lower = jnp.cumsum(lb_w, axis=0) - lb_w[0]

    n_gate = 4 * ML_HEADS
    g0 = 3 * 512
    w_in_p = jnp.concatenate([w_in[:, :, :g0], jnp.pad(w_in[:, :, g0:g0 + n_gate], ((0, 0), (0, 0), (0, LANES - n_gate))),
                              w_in[:, :, g0 + n_gate:]], axis=2).astype(BF16)
    w_out_b = w_out.astype(BF16)

    for l in range(depth):
        last = l == depth - 1
        qk, v, o, gates, qi, ff, fb, go = _inproj_call(xa, mod[l], norm1_g[l], w_in_p[l], tpb, n_batch)
        q, k, hq, hv = _conv_call(qk, qi, mlstm_conv[l], hgrn_conv[l], tpb)
        mlf, mlb, hgf, hgb = _mixers_call(q, k, v, gates, mlstm_gate_b[l], hq, hv, ff, fb, hgrn_f_b[l], lower[l],
                                          n_batch, nct)
        xa, h2, idx, gate = _out_call(xa, mlf, mlb, hgf, hgb, o, go, mod[l], mlstm_norm_g[l], hgrn_norm_g[l], w_out_b[l],
                                      norm2_g[l], router_w[l], router_b[l], tpb, n_batch)
        rank, cnt = _rank_call(idx)
        dest, meta = _dest_call(idx, rank, cnt, n_blocks)
        dest_flat = dest[:, :TOP_K].reshape(-1)
        zinfo = jnp.concatenate([meta[:N_EXPERTS, 2:4].reshape(-1), meta[0:1, 1]])
        xb = _scatter_call(dest_flat, zinfo, h2, n_rows)
        nblk = meta[:N_EXPERTS, 3] // MOE_BLK
        yb = _expert_call(l, meta[:n_blocks, 0], meta[0:1, 1], nblk, xb, w_gu, b_gu, w_down, b_down)
        xa = _combine_call(dest_flat, xa, gate, mod[l], final_g, yb, tpb, n_batch, last)
    return xa.reshape(n_batch, seq, d)
```
